```python
import math
import jax, jax.numpy as jnp
from jax import lax
import numpy as np

D_MODEL = 1024
BATCH = 8
SEQ = 8192
DEPTH = 2

N_MEM = 256
N_EVEN = (DEPTH + 1) // 2
N_ODD = DEPTH // 2
EPS = 1e-6

RET_HEADS = 6
RET_DK = 64
RET_DV = 128
RET_QK = RET_HEADS * RET_DK
RET_V = RET_HEADS * RET_DV
RET_CHUNK = 128
ROPE_THETA = 10000.0

S5_WIDTH = D_MODEL // 4
S5_GROUP = 16
S5_GROUPS = S5_WIDTH // S5_GROUP
S5_STATE = 64

EVEN_IN = 2 * RET_QK + 2 * RET_V + S5_WIDTH
EVEN_MIX = RET_V + S5_WIDTH

M2_DINNER = 2 * D_MODEL
M2_HEADDIM = 64
M2_HEADS = M2_DINNER // M2_HEADDIM
M2_GROUPS = 4
M2_HPG = M2_HEADS // M2_GROUPS
M2_STATE = 128
M2_CONV = 4
M2_CHUNK = 128
M2_CONV_DIM = M2_DINNER + 2 * M2_GROUPS * M2_STATE
M2_IN = M2_DINNER + M2_CONV_DIM + M2_HEADS

XA_HEADS = 4
XA_HEAD_DIM = D_MODEL // XA_HEADS

FFN_DENSE = 2816
N_EXPERTS = 8
TOP_K = 2
FFN_EXPERT = 3584
MOE_BLOCK = 1024

F32 = jnp.float32

kernel_name = 'hybrid_retention_s5_mamba2_moe'


def rmsnorm(x, w):
    xf = x.astype(F32)
    y = xf * lax.rsqrt(jnp.mean(xf * xf, axis=-1, keepdims=True) + EPS)
    return (y * w.astype(F32)).astype(x.dtype)


def to_chunks(a, size):
    b, l = a.shape[:2]
    a = a.reshape((b, l // size, size) + a.shape[2:])
    return jnp.moveaxis(a, 1, 0)


def from_chunks(a):
    nc, b, size = a.shape[:3]
    return jnp.moveaxis(a, 0, 1).reshape((b, nc * size) + a.shape[3:])


def rotary(a, positions):
    half = a.shape[-1] // 2
    inv_freq = ROPE_THETA ** (-jnp.arange(half, dtype=F32) / half)
    ang = positions.astype(F32)[:, :, None, None] * inv_freq
    cos, sin = jnp.cos(ang), jnp.sin(ang)
    a1, a2 = a[..., :half], a[..., half:]
    return jnp.concatenate([a1 * cos - a2 * sin, a1 * sin + a2 * cos], axis=-1)


def retention(q, k, v):
    b = q.shape[0]
    T = RET_CHUNK
    log_gamma = jnp.log1p(-(2.0 ** (-5.0 - jnp.arange(RET_HEADS, dtype=F32))))
    idx = jnp.arange(T, dtype=F32)
    diff = idx[:, None] - idx[None, :]
    decay_inner = jnp.where(diff[None] >= 0,
                            jnp.exp(jnp.maximum(diff, 0.0)[None] * log_gamma[:, None, None]), 0.0)
    decay_cross = jnp.exp((idx[:, None] + 1.0) * log_gamma)
    decay_state = jnp.exp((T - 1.0 - idx)[:, None] * log_gamma)
    decay_chunk = jnp.exp(T * log_gamma)

    def step(state, inp):
        qc, kc, vc = inp
        scores = jnp.einsum('bthd,bshd->bhts', qc, kc) * decay_inner[None]
        y = jnp.einsum('bhts,bshv->bthv', scores, vc)
        y = y + jnp.einsum('bthd,bhdv->bthv', qc, state) * decay_cross[None, :, :, None]
        state = state * decay_chunk[None, :, None, None] + jnp.einsum(
            'bshd,bshv->bhdv', kc * decay_state[None, :, :, None], vc)
        return state, y

    state0 = jnp.zeros((b, RET_HEADS, RET_DK, RET_DV), F32)
    _, y = lax.scan(step, state0, (to_chunks(q, T), to_chunks(k, T), to_chunks(v, T)))
    return from_chunks(y)


def s5_layer(u, lam_re, lam_im, log_dt, b_re, b_im, c_re, c_im, d_skip, w_glu, b_glu):
    bsz, l = u.shape[:2]
    uf = u.astype(F32).reshape(bsz, l, S5_GROUPS, S5_GROUP)
    dt = jnp.exp(log_dt.astype(F32))[:, None]
    lr, li = lam_re.astype(F32), lam_im.astype(F32)
    mag = jnp.exp(lr * dt)
    ab_re, ab_im = mag * jnp.cos(li * dt), mag * jnp.sin(li * dt)
    den = lr * lr + li * li
    nr, ni = ab_re - 1.0, ab_im
    f_re = (nr * lr + ni * li) / den
    f_im = (ni * lr - nr * li) / den
    br, bi = b_re.astype(F32), b_im.astype(F32)
    bb_re = f_re[..., None] * br - f_im[..., None] * bi
    bb_im = f_re[..., None] * bi + f_im[..., None] * br
    xr = jnp.einsum('blgc,gpc->lbgp', uf, bb_re)
    xi = jnp.einsum('blgc,gpc->lbgp', uf, bb_im)
    ar = jnp.broadcast_to(ab_re[None, None], xr.shape)
    ai = jnp.broadcast_to(ab_im[None, None], xr.shape)

    def combine(e1, e2):
        a1r, a1i, b1r, b1i = e1
        a2r, a2i, b2r, b2i = e2
        return (a2r * a1r - a2i * a1i, a2r * a1i + a2i * a1r,
                a2r * b1r - a2i * b1i + b2r, a2r * b1i + a2i * b1r + b2i)

    _, _, sr, si = lax.associative_scan(combine, (ar, ai, xr, xi), axis=0)
    y = jnp.einsum('lbgp,gcp->blgc', sr, c_re.astype(F32)) - jnp.einsum('lbgp,gcp->blgc', si, c_im.astype(F32))
    y = y.reshape(bsz, l, S5_WIDTH) + d_skip.astype(F32) * uf.reshape(bsz, l, S5_WIDTH)
    y = jax.nn.gelu(y)
    y = y * jax.nn.sigmoid(jnp.dot(y, w_glu.astype(F32)) + b_glu.astype(F32))
    return y.astype(u.dtype)


def even_mixer(h, positions, w_in, lam_re, lam_im, log_dt, b_re, b_im, c_re, c_im,
               d_skip, w_glu, b_glu, w_out):
    b, l, _ = h.shape
    proj = jnp.dot(h, w_in)
    q, k, v, g, u = jnp.split(
        proj, [RET_QK, 2 * RET_QK, 2 * RET_QK + RET_V, 2 * RET_QK + 2 * RET_V], axis=-1)
    q = rotary(q.reshape(b, l, RET_HEADS, RET_DK).astype(F32), positions)
    k = rotary(k.reshape(b, l, RET_HEADS, RET_DK).astype(F32), positions) * (RET_DK ** -0.5)
    v = v.reshape(b, l, RET_HEADS, RET_DV).astype(F32)
    y = retention(q, k, v)
    y = y * lax.rsqrt(jnp.mean(y * y, axis=-1, keepdims=True) + EPS)
    y_ret = (y.reshape(b, l, RET_V) * jax.nn.silu(g.astype(F32))).astype(h.dtype)
    y_s5 = s5_layer(u, lam_re, lam_im, log_dt, b_re, b_im, c_re, c_im, d_skip, w_glu, b_glu)
    return jnp.dot(jnp.concatenate([y_ret, y_s5], axis=-1), w_out)


def ssd_chunked(xdt, adt, bm, cm):
    b = xdt.shape[0]
    T = M2_CHUNK
    mask = jnp.tril(jnp.ones((T, T), dtype=bool))[None, :, :, None, None]

    def step(state, inp):
        xc, ac, bc, cc = inp
        acum = jnp.cumsum(ac, axis=1)
        seg = acum[:, :, None] - acum[:, None, :]
        L = jnp.exp(jnp.where(mask, seg, -jnp.inf))
        cb = jnp.einsum('btgn,bsgn->bgts', cc, bc)
        y_diag = jnp.einsum('bgts,btsgk,bsgkp->btgkp', cb, L, xc)
        y_off = jnp.einsum('btgn,bgkpn->btgkp', cc, state) * jnp.exp(acum)[..., None]
        decay = jnp.exp(acum[:, -1:] - acum)
        state = state * jnp.exp(acum[:, -1])[..., None, None] + jnp.einsum(
            'bsgn,bsgk,bsgkp->bgkpn', bc, decay, xc)
        return state, y_diag + y_off

    state0 = jnp.zeros((b, M2_GROUPS, M2_HPG, M2_HEADDIM, M2_STATE), F32)
    _, y = lax.scan(step, state0, (to_chunks(xdt, T), to_chunks(adt, T), to_chunks(bm, T), to_chunks(cm, T)))
    return from_chunks(y)


def mamba2_mixer(h, w_in, conv_w, conv_b, dt_bias, a_log, d_skip, norm_w, w_out):
    b, l, _ = h.shape
    proj = jnp.dot(h, w_in)
    z, xbc, dt = jnp.split(proj, [M2_DINNER, M2_DINNER + M2_CONV_DIM], axis=-1)
    xbc = lax.conv_general_dilated(xbc, conv_w[:, None, :], window_strides=(1,),
                                   padding=((M2_CONV - 1, 0),),
                                   dimension_numbers=('NWC', 'WIO', 'NWC'),
                                   feature_group_count=M2_CONV_DIM) + conv_b
    xbc = jax.nn.silu(xbc)
    xs, bm, cm = jnp.split(xbc, [M2_DINNER, M2_DINNER + M2_GROUPS * M2_STATE], axis=-1)
    xs = xs.reshape(b, l, M2_GROUPS, M2_HPG, M2_HEADDIM).astype(F32)
    bm = bm.reshape(b, l, M2_GROUPS, M2_STATE).astype(F32)
    cm = cm.reshape(b, l, M2_GROUPS, M2_STATE).astype(F32)
    dt = jax.nn.softplus(dt.astype(F32) + dt_bias.astype(F32)).reshape(b, l, M2_GROUPS, M2_HPG)
    a = -jnp.exp(a_log.astype(F32)).reshape(M2_GROUPS, M2_HPG)
    y = ssd_chunked(xs * dt[..., None], dt * a, bm, cm)
    y = y + d_skip.astype(F32).reshape(M2_GROUPS, M2_HPG)[:, :, None] * xs
    y = y.reshape(b, l, M2_DINNER) * jax.nn.silu(z.astype(F32))
    y = rmsnorm(y, norm_w).astype(h.dtype)
    return jnp.dot(y, w_out)


def cross_attention(h, mem_n, wq, wk, wv, wo):
    b, l, _ = h.shape
    q = jnp.dot(h, wq).reshape(b, l, XA_HEADS, XA_HEAD_DIM).astype(F32)
    k = jnp.dot(mem_n, wk).reshape(b, N_MEM, XA_HEADS, XA_HEAD_DIM).astype(F32)
    v = jnp.dot(mem_n, wv).reshape(b, N_MEM, XA_HEADS, XA_HEAD_DIM).astype(F32)
    p = jax.nn.softmax(jnp.einsum('bthd,bshd->bhts', q, k) * (XA_HEAD_DIM ** -0.5), axis=-1)
    o = jnp.einsum('bhts,bshd->bthd', p, v).reshape(b, l, D_MODEL).astype(h.dtype)
    return jnp.dot(o, wo)


def swiglu(h, w1, w3, w2):
    return jnp.dot(jax.nn.silu(jnp.dot(h, w1)) * jnp.dot(h, w3), w2)


def moe_swiglu(h, router, w1, w3, w2):
    b, l, d = h.shape
    n_tok = b * l
    blk = math.gcd(n_tok, MOE_BLOCK)
    ht = h.reshape(n_tok // blk, blk, d)

    def block(hb):
        logits = jnp.dot(hb, router).astype(F32)
        top_v, top_i = lax.top_k(logits, TOP_K)
        gates = jax.nn.softmax(top_v, axis=-1)
        comb = jnp.einsum('tk,tke->te', gates, jax.nn.one_hot(top_i, N_EXPERTS, dtype=F32))
        a = jnp.einsum('td,edf->etf', hb, w1)
        c = jnp.einsum('td,edf->etf', hb, w3)
        e_out = jnp.einsum('etf,efd->etd', jax.nn.silu(a) * c, w2)
        return jnp.einsum('te,etd->td', comb.astype(hb.dtype), e_out)

    return lax.map(block, ht).reshape(b, l, d)


def setup_inputs(seed: int = 0) -> dict:
    key = jax.random.key(seed)
    ks = iter(jax.random.split(key, 64))

    def nrm(shape, scale):
        return jax.random.normal(next(ks), shape, F32) * scale

    def gain(shape):
        return 1.0 + nrm(shape, 0.02)

    x = nrm((BATCH, SEQ, D_MODEL), 1.0)
    mem = nrm((BATCH, N_MEM, D_MODEL), 1.0)
    positions = (jax.random.randint(next(ks), (BATCH, 1), 0, 4096, dtype=jnp.int32)
                 + jnp.arange(SEQ, dtype=jnp.int32)[None, :])
    dt_m2 = jnp.exp(jax.random.uniform(next(ks), (N_ODD, M2_HEADS), F32, math.log(0.001), math.log(0.1)))
    return {
        'x': x,
        'mem': mem,
        'positions': positions,
        'mem_norm': gain((D_MODEL,)),
        'norm_mix': gain((DEPTH, D_MODEL)),
        'norm_xattn': gain((DEPTH, D_MODEL)),
        'norm_ffn': gain((DEPTH, D_MODEL)),
        'xa_wq': nrm((DEPTH, D_MODEL, D_MODEL), D_MODEL ** -0.5),
        'xa_wk': nrm((DEPTH, D_MODEL, D_MODEL), D_MODEL ** -0.5),
        'xa_wv': nrm((DEPTH, D_MODEL, D_MODEL), D_MODEL ** -0.5),
        'xa_wo': nrm((DEPTH, D_MODEL, D_MODEL), D_MODEL ** -0.5),
        'ev_w_in': nrm((N_EVEN, D_MODEL, EVEN_IN), D_MODEL ** -0.5),
        'ev_s5_lam_re': -0.5 + nrm((N_EVEN, S5_GROUPS, S5_STATE), 0.01),
        'ev_s5_lam_im': math.pi * jnp.arange(S5_STATE, dtype=F32) + nrm((N_EVEN, S5_GROUPS, S5_STATE), 0.01),
        'ev_s5_log_dt': jax.random.uniform(next(ks), (N_EVEN, S5_GROUPS), F32, math.log(0.001), math.log(0.1)),
        'ev_s5_b_re': nrm((N_EVEN, S5_GROUPS, S5_STATE, S5_GROUP), (2 * S5_GROUP) ** -0.5),
        'ev_s5_b_im': nrm((N_EVEN, S5_GROUPS, S5_STATE, S5_GROUP), (2 * S5_GROUP) ** -0.5),
        'ev_s5_c_re': nrm((N_EVEN, S5_GROUPS, S5_GROUP, S5_STATE), (2 * S5_STATE) ** -0.5),
        'ev_s5_c_im': nrm((N_EVEN, S5_GROUPS, S5_GROUP, S5_STATE), (2 * S5_STATE) ** -0.5),
        'ev_s5_d': nrm((N_EVEN, S5_WIDTH), 1.0),
        'ev_s5_w_glu': nrm((N_EVEN, S5_WIDTH, S5_WIDTH), S5_WIDTH ** -0.5),
        'ev_s5_b_glu': nrm((N_EVEN, S5_WIDTH), 0.01),
        'ev_w_out': nrm((N_EVEN, EVEN_MIX, D_MODEL), EVEN_MIX ** -0.5),
        'ev_ffn_w1': nrm((N_EVEN, D_MODEL, FFN_DENSE), D_MODEL ** -0.5),
        'ev_ffn_w3': nrm((N_EVEN, D_MODEL, FFN_DENSE), D_MODEL ** -0.5),
        'ev_ffn_w2': nrm((N_EVEN, FFN_DENSE, D_MODEL), FFN_DENSE ** -0.5),
        'od_w_in': nrm((N_ODD, D_MODEL, M2_IN), D_MODEL ** -0.5),
        'od_conv_w': nrm((N_ODD, M2_CONV, M2_CONV_DIM), M2_CONV ** -0.5),
        'od_conv_b': nrm((N_ODD, M2_CONV_DIM), 0.01),
        'od_dt_bias': dt_m2 + jnp.log(-jnp.expm1(-dt_m2)),
        'od_a_log': jnp.log(jax.random.uniform(next(ks), (N_ODD, M2_HEADS), F32, 1.0, 16.0)),
        'od_d': 1.0 + nrm((N_ODD, M2_HEADS), 0.1),
        'od_norm': gain((N_ODD, M2_DINNER)),
        'od_w_out': nrm((N_ODD, M2_DINNER, D_MODEL), M2_DINNER ** -0.5),
        'od_router': nrm((N_ODD, D_MODEL, N_EXPERTS), D_MODEL ** -0.5),
        'od_moe_w1': nrm((N_ODD, N_EXPERTS, D_MODEL, FFN_EXPERT), D_MODEL ** -0.5),
        'od_moe_w3': nrm((N_ODD, N_EXPERTS, D_MODEL, FFN_EXPERT), D_MODEL ** -0.5),
        'od_moe_w2': nrm((N_ODD, N_EXPERTS, FFN_EXPERT, D_MODEL), FFN_EXPERT ** -0.5),
        'final_norm': gain((D_MODEL,)),
    }


def reference(x, mem, positions, mem_norm, norm_mix, norm_xattn, norm_ffn,
              xa_wq, xa_wk, xa_wv, xa_wo,
              ev_w_in, ev_s5_lam_re, ev_s5_lam_im, ev_s5_log_dt, ev_s5_b_re, ev_s5_b_im,
              ev_s5_c_re, ev_s5_c_im, ev_s5_d, ev_s5_w_glu, ev_s5_b_glu, ev_w_out,
              ev_ffn_w1, ev_ffn_w3, ev_ffn_w2,
              od_w_in, od_conv_w, od_conv_b, od_dt_bias, od_a_log, od_d, od_norm, od_w_out,
              od_router, od_moe_w1, od_moe_w3, od_moe_w2, final_norm):
    mem_n = rmsnorm(mem, mem_norm)
    h = x
    for layer in range(DEPTH):
        i = layer // 2
        hn = rmsnorm(h, norm_mix[layer])
        if layer % 2 == 0:
            h = h + even_mixer(hn, positions, ev_w_in[i], ev_s5_lam_re[i], ev_s5_lam_im[i],
                               ev_s5_log_dt[i], ev_s5_b_re[i], ev_s5_b_im[i], ev_s5_c_re[i],
                               ev_s5_c_im[i], ev_s5_d[i], ev_s5_w_glu[i], ev_s5_b_glu[i], ev_w_out[i])
        else:
            h = h + mamba2_mixer(hn, od_w_in[i], od_conv_w[i], od_conv_b[i], od_dt_bias[i],
                                 od_a_log[i], od_d[i], od_norm[i], od_w_out[i])
        h = h + cross_attention(rmsnorm(h, norm_xattn[layer]), mem_n,
                                xa_wq[layer], xa_wk[layer], xa_wv[layer], xa_wo[layer])
        hn = rmsnorm(h, norm_ffn[layer])
        if layer % 2 == 0:
            h = h + swiglu(hn, ev_ffn_w1[i], ev_ffn_w3[i], ev_ffn_w2[i])
        else:
            h = h + moe_swiglu(hn, od_router[i], od_moe_w1[i], od_moe_w3[i], od_moe_w2[i])
    return rmsnorm(h, final_norm)
```

```python
import functools
import math

import jax
import jax.numpy as jnp
from jax import lax
from jax.experimental import pallas as pl
from jax.experimental.pallas import tpu as pltpu

F32 = jnp.float32
BF16 = jnp.bfloat16
EPS = 1e-6

LANES = 128
SUBLANES = 8
VMEM_LIMIT_BYTES = 56 * 1024 * 1024

RET_HEADS = 6
RET_DK = 64
RET_DV = 128
RET_QK = RET_HEADS * RET_DK
RET_V = RET_HEADS * RET_DV
RET_CHUNK = 128
ROPE_THETA = 10000.0
S5_GROUP = 16
S5_GROUPS = 16
S5_STATE = 64
S5_WIDTH = S5_GROUP * S5_GROUPS
S5_STEPS = 64
M2_HEADDIM = 64
M2_HEADS = 32
M2_GROUPS = 4
M2_HPG = M2_HEADS // M2_GROUPS
M2_STATE = 128
M2_CONV = 4
M2_CHUNK = 128
M2_DINNER = M2_HEADS * M2_HEADDIM
M2_BC = M2_GROUPS * M2_STATE
XA_HEADS = 4
N_EXPERTS = 8
ROW_TILE = 512


def _params(semantics):
    return pltpu.CompilerParams(dimension_semantics=semantics,
                                vmem_limit_bytes=VMEM_LIMIT_BYTES)


def _rms(x, w):
    return x * lax.rsqrt(jnp.mean(x * x, axis=-1, keepdims=True) + EPS) * w


def _dot(a, b):
    return jnp.dot(a, b, preferred_element_type=F32)


def _split_bf16(x):
    hi = x.astype(BF16)
    lo = (x - hi.astype(F32)).astype(BF16)
    return hi, lo


def _silu(x):
    return x * (1.0 / (1.0 + jnp.exp(-x)))


def _memkv_kernel(mem_ref, nw_ref, wk_ref, wv_ref, k_ref, v_ref):
    m = _rms(mem_ref[0], nw_ref[...]).astype(BF16)
    k_ref[0, 0] = _dot(m, wk_ref[0]).astype(BF16)
    v_ref[0, 0] = _dot(m, wv_ref[0]).astype(BF16)


def _memkv(mem, mem_norm, wk, wv):
    b, nm, d = mem.shape
    depth = wk.shape[0]
    out = jax.ShapeDtypeStruct((depth, b, nm, d), BF16)
    return pl.pallas_call(
        _memkv_kernel,
        out_shape=(out, out),
        grid=(depth, b),
        in_specs=[
            pl.BlockSpec((1, nm, d), lambda l, i: (i, 0, 0)),
            pl.BlockSpec((1, d), lambda l, i: (0, 0)),
            pl.BlockSpec((1, d, d), lambda l, i: (l, 0, 0)),
            pl.BlockSpec((1, d, d), lambda l, i: (l, 0, 0)),
        ],
        out_specs=(pl.BlockSpec((1, 1, nm, d), lambda l, i: (l, i, 0, 0)),
                   pl.BlockSpec((1, 1, nm, d), lambda l, i: (l, i, 0, 0))),
        compiler_params=_params(("arbitrary", "arbitrary")),
        name="memkv",
    )(mem, mem_norm.reshape(1, d), wk, wv)


def _in_proj_kernel(bounds, x_ref, nw_ref, w_ref, *out_refs):
    hn = _rms(x_ref[...], nw_ref[...]).astype(BF16)
    for (lo, hi), o_ref in zip(bounds, out_refs):
        o_ref[...] = _dot(hn, w_ref[:, lo:hi]).astype(o_ref.dtype)


def _in_proj(x, norm_w, w, widths, dtypes):
    n, d = x.shape
    bounds, lo = [], 0
    for wd in widths:
        bounds.append((lo, lo + wd))
        lo += wd
    assert lo == w.shape[1]
    tm = ROW_TILE
    return pl.pallas_call(
        functools.partial(_in_proj_kernel, tuple(bounds)),
        out_shape=tuple(jax.ShapeDtypeStruct((n, wd), dt) for wd, dt in zip(widths, dtypes)),
        grid=(n // tm,),
        in_specs=[
            pl.BlockSpec((tm, d), lambda i: (i, 0)),
            pl.BlockSpec((1, d), lambda i: (0, 0)),
            pl.BlockSpec(w.shape, lambda i: (0, 0)),
        ],
        out_specs=tuple(pl.BlockSpec((tm, wd), lambda i: (i, 0)) for wd in widths),
        compiler_params=_params(("arbitrary",)),
        name="in_proj",
    )(x, norm_w.reshape(1, d), w)


def _ret_kernel(q_ref, k_ref, v_ref, g_ref, pos_ref, invf_ref, rot_ref, din_ref,
                dcr_ref, dst_ref, dch_ref, o_ref, state_ref):
    @pl.when(pl.program_id(1) == 0)
    def _():
        state_ref[...] = jnp.zeros_like(state_ref)

    ang = pos_ref[0].astype(F32) * invf_ref[...]
    cos = jnp.cos(ang)
    sin = jnp.sin(ang)
    reps = RET_QK // LANES
    cos = jnp.concatenate([cos] * reps, axis=1)
    sin = jnp.concatenate([sin] * reps, axis=1)

    def rotary(x):
        hi, lo = _split_bf16(x)
        swapped = _dot(hi, rot_ref[...]) + _dot(lo, rot_ref[...])
        return x * cos + swapped * sin

    q = rotary(q_ref[0]).astype(BF16)
    kf = rotary(k_ref[0]) * (RET_DK ** -0.5)
    k = kf.astype(BF16)
    kd = (kf * dst_ref[...]).astype(BF16)
    v = v_ref[0]
    g = g_ref[0]
    state = state_ref[...]
    state_bf = state.astype(BF16)
    dcr = dcr_ref[...]
    for h in range(RET_HEADS):
        ks = slice(h * RET_DK, (h + 1) * RET_DK)
        vs = slice(h * RET_DV, (h + 1) * RET_DV)
        scores = lax.dot_general(q[:, ks], k[:, ks], (((1,), (1,)), ((), ())),
                                 preferred_element_type=F32) * din_ref[h]
        y = _dot(scores.astype(BF16), v[:, vs])
        y = y + _dot(q[:, ks], state_bf[:, vs]) * dcr[:, vs]
        upd = lax.dot_general(kd[:, ks], v[:, vs], (((0,), (0,)), ((), ())),
                              preferred_element_type=F32)
        state_ref[:, vs] = state[:, vs] * dch_ref[:, vs] + upd
        y = y * lax.rsqrt(jnp.mean(y * y, axis=-1, keepdims=True) + EPS)
        o_ref[0, :, vs] = (y * _silu(g[:, vs])).astype(o_ref.dtype)


def _retention_tables():
    t = RET_CHUNK
    log_gamma = jnp.log1p(-(2.0 ** (-5.0 - jnp.arange(RET_HEADS, dtype=F32))))
    idx = jnp.arange(t, dtype=F32)
    diff = idx[:, None] - idx[None, :]
    d_inner = jnp.where(diff[None] >= 0,
                        jnp.exp(jnp.maximum(diff, 0.0)[None] * log_gamma[:, None, None]), 0.0)
    d_cross = jnp.exp((idx[:, None] + 1.0) * log_gamma)
    d_state = jnp.exp((t - 1.0 - idx)[:, None] * log_gamma)
    d_chunk = jnp.exp(t * log_gamma)
    d_cross = jnp.repeat(d_cross, RET_DV, axis=1)
    d_state = jnp.repeat(d_state, RET_DK, axis=1)
    d_chunk = jnp.repeat(d_chunk, RET_DV)[None, :]
    half = RET_DK // 2
    inv_freq = ROPE_THETA ** (-jnp.arange(half, dtype=F32) / half)
    inv_freq = jnp.tile(inv_freq, LANES // half)[None, :]
    col = jnp.arange(RET_QK)
    first = (col % RET_DK) < half
    src = jnp.where(first, col + half, col - half)
    rot = jnp.zeros((RET_QK, RET_QK), F32).at[src, col].set(jnp.where(first, -1.0, 1.0))
    return inv_freq, rot.astype(BF16), d_inner, d_cross, d_state, d_chunk


def _retention(q, k, v, g, positions):
    b, l, _ = q.shape
    t = RET_CHUNK
    inv_freq, rot, d_inner, d_cross, d_state, d_chunk = _retention_tables()
    seq = lambda w: pl.BlockSpec((1, t, w), lambda i, c: (i, c, 0))
    const2 = lambda a: pl.BlockSpec(a.shape, lambda i, c: (0, 0))
    return pl.pallas_call(
        _ret_kernel,
        out_shape=jax.ShapeDtypeStruct((b, l, RET_V), BF16),
        grid=(b, l // t),
        in_specs=[seq(RET_QK), seq(RET_QK), seq(RET_V), seq(RET_V), seq(1),
                  const2(inv_freq), const2(rot),
                  pl.BlockSpec(d_inner.shape, lambda i, c: (0, 0, 0)),
                  const2(d_cross), const2(d_state), const2(d_chunk)],
        out_specs=seq(RET_V),
        scratch_shapes=[pltpu.VMEM((RET_DK, RET_V), F32)],
        compiler_params=_params(("arbitrary", "arbitrary")),
        name="retention",
    )(q, k, v, g, positions.reshape(b, l, 1), inv_freq, rot, d_inner, d_cross, d_state, d_chunk)


def _s5_kernel(u_ref, bin_ref, are_ref, aim_ref, cout_ref, d_ref, wg_ref, bg_ref,
               o_ref, x_scr, st_ref):
    @pl.when(pl.program_id(0) == 0)
    def _():
        st_ref[...] = jnp.zeros_like(st_ref)

    nstate = S5_GROUPS * S5_STATE
    u = u_ref[...]
    x_scr[...] = _dot(u.astype(BF16), bin_ref[...])
    rows = st_ref.shape[1]
    are = jnp.broadcast_to(are_ref[...], (rows, nstate))
    aim = jnp.broadcast_to(aim_ref[...], (rows, nstate))

    def step(t, carry):
        sr, si = carry
        r0 = pl.multiple_of(t * rows, rows)
        xr = x_scr[pl.ds(r0, rows), :nstate]
        xi = x_scr[pl.ds(r0, rows), nstate:]
        nr = are * sr - aim * si + xr
        ni = are * si + aim * sr + xi
        x_scr[pl.ds(r0, rows), :nstate] = nr
        x_scr[pl.ds(r0, rows), nstate:] = ni
        return nr, ni

    sr, si = lax.fori_loop(0, x_scr.shape[0] // rows, step, (st_ref[0], st_ref[1]))
    st_ref[0] = sr
    st_ref[1] = si
    hi, lo = _split_bf16(x_scr[...])
    y = _dot(hi, cout_ref[...]) + _dot(lo, cout_ref[...]) + d_ref[...] * u
    y = jax.nn.gelu(y)
    gate = _dot(y.astype(BF16), wg_ref[...]) + bg_ref[...]
    o_ref[...] = (y * (1.0 / (1.0 + jnp.exp(-gate)))).astype(o_ref.dtype)


def _s5_tables(lam_re, lam_im, log_dt, b_re, b_im, c_re, c_im):
    dt = jnp.exp(log_dt)[:, None]
    mag = jnp.exp(lam_re * dt)
    ab_re, ab_im = mag * jnp.cos(lam_im * dt), mag * jnp.sin(lam_im * dt)
    den = lam_re * lam_re + lam_im * lam_im
    nr, ni = ab_re - 1.0, ab_im
    f_re = (nr * lam_re + ni * lam_im) / den
    f_im = (ni * lam_re - nr * lam_im) / den
    bb_re = f_re[..., None] * b_re - f_im[..., None] * b_im
    bb_im = f_re[..., None] * b_im + f_im[..., None] * b_re
    eye = jnp.eye(S5_GROUPS, dtype=F32)
    bd = lambda m: jnp.einsum('gpc,gh->gchp', m, eye).reshape(S5_WIDTH, S5_GROUPS * S5_STATE)
    b_in = jnp.concatenate([bd(bb_re), bd(bb_im)], axis=1)
    cd = lambda m: jnp.einsum('gcp,gh->gphc', m, eye).reshape(S5_GROUPS * S5_STATE, S5_WIDTH)
    c_out = jnp.concatenate([cd(c_re), -cd(c_im)], axis=0)
    return (b_in.astype(BF16), ab_re.reshape(1, -1), ab_im.reshape(1, -1), c_out.astype(BF16))


def _s5(u_tb, batch, tables, d_skip, w_glu, b_glu):
    rows_total, width = u_tb.shape
    b_in, a_re, a_im, c_out = tables
    nstate = S5_GROUPS * S5_STATE
    tr = S5_STEPS * batch
    const = lambda a: pl.BlockSpec(a.shape, lambda i: (0, 0))
    d_skip = d_skip.reshape(1, width)
    b_glu = b_glu.reshape(1, width)
    return pl.pallas_call(
        _s5_kernel,
        out_shape=jax.ShapeDtypeStruct((rows_total, width), BF16),
        grid=(rows_total // tr,),
        in_specs=[pl.BlockSpec((tr, width), lambda i: (i, 0)),
                  const(b_in), const(a_re), const(a_im), const(c_out),
                  const(d_skip), const(w_glu), const(b_glu)],
        out_specs=pl.BlockSpec((tr, width), lambda i: (i, 0)),
        scratch_shapes=[pltpu.VMEM((tr, 2 * nstate), F32),
                        pltpu.VMEM((2, batch, nstate), F32)],
        compiler_params=_params(("arbitrary",)),
        name="s5",
    )(u_tb, b_in, a_re, a_im, c_out, d_skip, w_glu, b_glu)


def _out_xattn_kernel(n_y, *refs):
    res_ref = refs[0]
    y_refs = refs[1:1 + n_y]
    w_refs = refs[1 + n_y:1 + 2 * n_y]
    nx_ref, wq_ref, k_ref, v_ref, wo_ref, o_ref = refs[1 + 2 * n_y:]
    h = res_ref[...]
    for y_ref, w_ref in zip(y_refs, w_refs):
        h = h + _dot(y_ref[...], w_ref[...])
    d = h.shape[1]
    hd = d // XA_HEADS
    q = (_dot(_rms(h, nx_ref[...]).astype(BF16), wq_ref[...]) * (hd ** -0.5)).astype(BF16)
    k = k_ref[0]
    v = v_ref[0]
    outs = []
    for i in range(XA_HEADS):
        hs = slice(i * hd, (i + 1) * hd)
        s = lax.dot_general(q[:, hs], k[:, hs], (((1,), (1,)), ((), ())),
                            preferred_element_type=F32)
        p = jnp.exp(s - jnp.max(s, axis=-1, keepdims=True))
        p = p / jnp.sum(p, axis=-1, keepdims=True)
        outs.append(_dot(p.astype(BF16), v[:, hs]).astype(BF16))
    o = jnp.concatenate(outs, axis=1)
    o_ref[...] = h + _dot(o, wo_ref[...])


def _out_xattn(res, ys, ws, norm_w, wq, k, v, wo, seq_len):
    n, d = res.shape
    tm = ROW_TILE
    nm = k.shape[1]
    tiles_per_seq = seq_len // tm
    row = lambda w: pl.BlockSpec((tm, w), lambda i: (i, 0))
    const = lambda a: pl.BlockSpec(a.shape, lambda i: (0, 0))
    kv = pl.BlockSpec((1, nm, d), lambda i: (i // tiles_per_seq, 0, 0))
    return pl.pallas_call(
        functools.partial(_out_xattn_kernel, len(ys)),
        out_shape=jax.ShapeDtypeStruct((n, d), F32),
        grid=(n // tm,),
        in_specs=[row(d)] + [row(y.shape[1]) for y in ys] + [const(w) for w in ws]
                 + [pl.BlockSpec((1, d), lambda i: (0, 0)), const(wq), kv, kv, const(wo)],
        out_specs=row(d),
        compiler_params=_params(("arbitrary",)),
        name="out_xattn",
    )(res, *ys, *ws, norm_w.reshape(1, d), wq, k, v, wo)


def _ffn_kernel(h_ref, nw_ref, w1_ref, w3_ref, w2_ref, o_ref, hn_scr):
    f = pl.program_id(1)

    @pl.when(f == 0)
    def _():
        h = h_ref[...]
        hn_scr[...] = _rms(h, nw_ref[...]).astype(BF16)
        o_ref[...] = h

    hn = hn_scr[...]
    a = _dot(hn, w1_ref[...])
    c = _dot(hn, w3_ref[...])
    o_ref[...] += _dot((_silu(a) * c).astype(BF16), w2_ref[...])


def _ffn(h, norm_w, w1, w3, w2, n_chunks):
    n, d = h.shape
    f = w1.shape[1]
    fc = f // n_chunks
    tm = ROW_TILE
    return pl.pallas_call(
        _ffn_kernel,
        out_shape=jax.ShapeDtypeStruct((n, d), F32),
        grid=(n // tm, n_chunks),
        in_specs=[pl.BlockSpec((tm, d), lambda i, j: (i, 0)),
                  pl.BlockSpec((1, d), lambda i, j: (0, 0)),
                  pl.BlockSpec((d, fc), lambda i, j: (0, j)),
                  pl.BlockSpec((d, fc), lambda i, j: (0, j)),
                  pl.BlockSpec((fc, d), lambda i, j: (j, 0))],
        out_specs=pl.BlockSpec((tm, d), lambda i, j: (i, 0)),
        scratch_shapes=[pltpu.VMEM((tm, d), BF16)],
        compiler_params=_params(("arbitrary", "arbitrary")),
        name="ffn",
    )(h, norm_w.reshape(1, d), w1, w3, w2)


def _softplus(x):
    return jnp.maximum(x, 0.0) + jnp.log1p(jnp.exp(-jnp.abs(x)))


def _ssd_kernel(z_ref, xbc_ref, dt_ref, dtt_ref, cw_ref, cb_ref, dtb_ref, dtbt_ref,
                alog_ref, alogt_ref, dsk_ref, nw_ref, o_ref, xpad_scr, state_scr, y_scr):
    t = M2_CHUNK
    pad = SUBLANES

    @pl.when(pl.program_id(1) == 0)
    def _():
        xpad_scr[0:pad, :] = jnp.zeros((pad, xpad_scr.shape[1]), F32)
        state_scr[...] = jnp.zeros_like(state_scr)

    xbc = xbc_ref[0].astype(F32)
    xpad_scr[pad:pad + t, :] = xbc
    acc = cb_ref[...] + cw_ref[0:1, :] * xpad_scr[pl.ds(pad - M2_CONV + 1, t), :]
    for j in range(1, M2_CONV):
        acc = acc + cw_ref[j:j + 1, :] * xpad_scr[pl.ds(pad - M2_CONV + 1 + j, t), :]
    xpad_scr[0:pad, :] = xbc[t - pad:, :]
    xc = _silu(acc)
    xs = xc[:, :M2_DINNER]
    xs_bf = xs.astype(BF16)
    bm = xc[:, M2_DINNER:M2_DINNER + M2_BC].astype(BF16)
    cm = xc[:, M2_DINNER + M2_BC:]
    cm_bf = cm.astype(BF16)

    dt = _softplus(dt_ref[0] + dtb_ref[...])
    adt = dt * (-jnp.exp(alog_ref[...]))
    dtt = _softplus(dtt_ref[0] + dtbt_ref[...])
    adtt = dtt * (-jnp.exp(alogt_ref[...]))
    ri = lax.broadcasted_iota(jnp.int32, (t, t), 0)
    ci = lax.broadcasted_iota(jnp.int32, (t, t), 1)
    causal = ri >= ci
    tri = causal.astype(F32)
    acum = jnp.dot(tri, adt, preferred_element_type=F32, precision=lax.Precision.HIGHEST)
    acumt = lax.dot_general(adtt, tri, (((1,), (1,)), ((), ())), preferred_element_type=F32,
                            precision=lax.Precision.HIGHEST)
    alast = acum[t - 1:t, :]
    wdec = dt * jnp.exp(alast - acum)
    eac = jnp.exp(acum)
    sdec = jnp.exp(alast)

    state = state_scr[...]
    state_bf = state.astype(BF16)
    for g in range(M2_GROUPS):
        ns = slice(g * M2_STATE, (g + 1) * M2_STATE)
        cb = lax.dot_general(cm_bf[:, ns], bm[:, ns], (((1,), (1,)), ((), ())),
                             preferred_element_type=F32)
        wcols, scols = [], []
        for kh in range(M2_HPG):
            h = g * M2_HPG + kh
            ps = slice(h * M2_HEADDIM, (h + 1) * M2_HEADDIM)
            seg = acum[:, h:h + 1] - acumt[h:h + 1, :]
            lmat = jnp.exp(jnp.where(causal, seg, -jnp.inf))
            m = (cb * lmat * dtt[h:h + 1, :]).astype(BF16)
            coff = (cm[:, ns] * eac[:, h:h + 1]).astype(BF16)
            lhs = jnp.concatenate([m, coff], axis=1)
            rhs = jnp.concatenate([xs_bf[:, ps], state_bf[:, ps]], axis=0)
            y = _dot(lhs, rhs) + dsk_ref[:, h:h + 1] * xs[:, ps]
            y_scr[:, ps] = y
            wcols.append(jnp.broadcast_to(wdec[:, h:h + 1], (t, M2_HEADDIM)))
            scols.append(jnp.broadcast_to(sdec[:, h:h + 1], (1, M2_HEADDIM)))
        gs = slice(g * M2_HPG * M2_HEADDIM, (g + 1) * M2_HPG * M2_HEADDIM)
        xd = (xs[:, gs] * jnp.concatenate(wcols, axis=1)).astype(BF16)
        upd = lax.dot_general(bm[:, ns], xd, (((0,), (0,)), ((), ())),
                              preferred_element_type=F32)
        state_scr[:, gs] = state[:, gs] * jnp.concatenate(scols, axis=1) + upd

    yz = y_scr[...] * _silu(z_ref[0].astype(F32))
    o_ref[0] = _rms(yz, nw_ref[...]).astype(o_ref.dtype)


def _ssd(z, xbc, dt, conv_w, conv_b, dt_bias, a_log, d_skip, norm_w):
    b, l, _ = z.shape
    t = M2_CHUNK
    cdim = xbc.shape[2]
    dtt = jnp.swapaxes(dt, 1, 2)
    row = lambda a: a.reshape(1, -1)
    col = lambda a: a.reshape(-1, 1)
    seq = lambda w: pl.BlockSpec((1, t, w), lambda i, c: (i, c, 0))
    const = lambda a: pl.BlockSpec(a.shape, lambda i, c: (0, 0))
    args = [z, xbc, dt, dtt, conv_w, row(conv_b), row(dt_bias), col(dt_bias),
            row(a_log), col(a_log), row(d_skip), row(norm_w)]
    in_specs = [seq(M2_DINNER), seq(cdim), seq(M2_HEADS),
                pl.BlockSpec((1, M2_HEADS, t), lambda i, c: (i, 0, c))]
    in_specs += [const(a) for a in args[4:]]
    return pl.pallas_call(
        _ssd_kernel,
        out_shape=jax.ShapeDtypeStruct((b, l, M2_DINNER), BF16),
        grid=(b, l // t),
        in_specs=in_specs,
        out_specs=seq(M2_DINNER),
        scratch_shapes=[pltpu.VMEM((SUBLANES + t, cdim), F32),
                        pltpu.VMEM((M2_STATE, M2_DINNER), F32),
                        pltpu.VMEM((t, M2_DINNER), F32)],
        compiler_params=_params(("arbitrary", "arbitrary")),
        name="ssd",
    )(*args)


def _moe_kernel(h_ref, nw_ref, r_ref, w1_ref, w3_ref, w2_ref, fw_ref, o_ref, hn_scr, comb_scr):
    e = pl.program_id(1)
    f = pl.program_id(2)
    first = jnp.logical_and(e == 0, f == 0)
    last = jnp.logical_and(e == pl.num_programs(1) - 1, f == pl.num_programs(2) - 1)
    lane = lax.broadcasted_iota(jnp.int32, comb_scr.shape, 1).astype(F32)

    @pl.when(first)
    def _():
        h = h_ref[...]
        hn = _rms(h, nw_ref[...])
        hn_scr[...] = hn.astype(BF16)
        o_ref[...] = h
        logits = jnp.dot(hn, r_ref[...], preferred_element_type=F32,
                         precision=lax.Precision.HIGHEST)
        logits = jnp.where(lane < N_EXPERTS, logits, -jnp.inf)
        m1 = jnp.max(logits, axis=-1, keepdims=True)
        i1 = jnp.min(jnp.where(logits == m1, lane, float(LANES)), axis=-1, keepdims=True)
        rest = jnp.where(lane == i1, -jnp.inf, logits)
        m2 = jnp.max(rest, axis=-1, keepdims=True)
        i2 = jnp.min(jnp.where(rest == m2, lane, float(LANES)), axis=-1, keepdims=True)
        e2 = jnp.exp(m2 - m1)
        g1 = 1.0 / (1.0 + e2)
        g2 = e2 / (1.0 + e2)
        comb_scr[...] = jnp.where(lane == i1, g1, 0.0) + jnp.where(lane == i2, g2, 0.0)

    gate = jnp.sum(jnp.where(lane == e.astype(F32), comb_scr[...], 0.0), axis=-1, keepdims=True)
    hn = hn_scr[...]
    a = _dot(hn, w1_ref[0])
    c = _dot(hn, w3_ref[0])
    o_ref[...] += gate * _dot((_silu(a) * c).astype(BF16), w2_ref[0])

    @pl.when(last)
    def _():
        o_ref[...] = _rms(o_ref[...], fw_ref[...])


def _moe(h, norm_w, router, w1, w3, w2, final_w, n_chunks):
    n, d = h.shape
    ne, _, f = w1.shape
    fc = f // n_chunks
    tm = ROW_TILE
    router_p = jnp.zeros((d, LANES), F32).at[:, :ne].set(router)
    return pl.pallas_call(
        _moe_kernel,
        out_shape=jax.ShapeDtypeStruct((n, d), F32),
        grid=(n // tm, ne, n_chunks),
        in_specs=[pl.BlockSpec((tm, d), lambda i, e, j: (i, 0)),
                  pl.BlockSpec((1, d), lambda i, e, j: (0, 0)),
                  pl.BlockSpec((d, LANES), lambda i, e, j: (0, 0)),
                  pl.BlockSpec((1, d, fc), lambda i, e, j: (e, 0, j)),
                  pl.BlockSpec((1, d, fc), lambda i, e, j: (e, 0, j)),
                  pl.BlockSpec((1, fc, d), lambda i, e, j: (e, j, 0)),
                  pl.BlockSpec((1, d), lambda i, e, j: (0, 0))],
        out_specs=pl.BlockSpec((tm, d), lambda i, e, j: (i, 0)),
        scratch_shapes=[pltpu.VMEM((tm, d), BF16), pltpu.VMEM((tm, LANES), F32)],
        compiler_params=_params(("arbitrary", "arbitrary", "arbitrary")),
        name="moe",
    )(h, norm_w.reshape(1, d), router_p, w1, w3, w2, final_w.reshape(1, d))


def kernel(x, mem, positions, mem_norm, norm_mix, norm_xattn, norm_ffn, xa_wq, xa_wk, xa_wv, xa_wo, ev_w_in, ev_s5_lam_re, ev_s5_lam_im, ev_s5_log_dt, ev_s5_b_re, ev_s5_b_im, ev_s5_c_re, ev_s5_c_im, ev_s5_d, ev_s5_w_glu, ev_s5_b_glu, ev_w_out, ev_ffn_w1, ev_ffn_w3, ev_ffn_w2, od_w_in, od_conv_w, od_conv_b, od_dt_bias, od_a_log, od_d, od_norm, od_w_out, od_router, od_moe_w1, od_moe_w3, od_moe_w2, final_norm):
    b, l, d = x.shape
    n = b * l
    assert l % ROW_TILE == 0 and l % RET_CHUNK == 0 and l % M2_CHUNK == 0 and l % S5_STEPS == 0
    assert b == SUBLANES, "the S5 scan maps the batch onto the sublanes"
    bf = lambda a: a.astype(BF16)

    mem_k, mem_v = _memkv(mem, mem_norm, bf(xa_wk), bf(xa_wv))
    h = x.reshape(n, d)

    q, k, v, g, u = _in_proj(
        h, norm_mix[0], bf(ev_w_in[0]),
        (RET_QK, RET_QK, RET_V, RET_V, S5_WIDTH), (F32, F32, BF16, F32, F32))
    y_ret = _retention(q.reshape(b, l, -1), k.reshape(b, l, -1), v.reshape(b, l, -1),
                       g.reshape(b, l, -1), positions)
    u_tb = jnp.swapaxes(u.reshape(b, l, S5_WIDTH), 0, 1).reshape(n, S5_WIDTH)
    tables = _s5_tables(ev_s5_lam_re[0], ev_s5_lam_im[0], ev_s5_log_dt[0], ev_s5_b_re[0],
                        ev_s5_b_im[0], ev_s5_c_re[0], ev_s5_c_im[0])
    y_s5 = _s5(u_tb, b, tables, ev_s5_d[0], bf(ev_s5_w_glu[0]), ev_s5_b_glu[0])
    y_s5 = jnp.swapaxes(y_s5.reshape(l, b, S5_WIDTH), 0, 1).reshape(n, S5_WIDTH)
    w_out = bf(ev_w_out[0])
    h = _out_xattn(h, [y_ret.reshape(n, RET_V), y_s5], [w_out[:RET_V], w_out[RET_V:]],
                   norm_xattn[0], bf(xa_wq[0]), mem_k[0], mem_v[0], bf(xa_wo[0]), l)
    h = _ffn(h, norm_ffn[0], bf(ev_ffn_w1[0]), bf(ev_ffn_w3[0]), bf(ev_ffn_w2[0]), 2)

    cdim = od_conv_w.shape[2]
    z, xbc, dt = _in_proj(h, norm_mix[1], bf(od_w_in[0]),
                          (M2_DINNER, cdim, M2_HEADS), (BF16, BF16, F32))
    y = _ssd(z.reshape(b, l, -1), xbc.reshape(b, l, -1), dt.reshape(b, l, -1),
             od_conv_w[0], od_conv_b[0], od_dt_bias[0], od_a_log[0], od_d[0], od_norm[0])
    h = _out_xattn(h, [y.reshape(n, M2_DINNER)], [bf(od_w_out[0])],
                   norm_xattn[1], bf(xa_wq[1]), mem_k[1], mem_v[1], bf(xa_wo[1]), l)
    out = _moe(h, norm_ffn[1], od_router[0], bf(od_moe_w1[0]), bf(od_moe_w3[0]),
               bf(od_moe_w2[0]), final_norm, 2)
    return out.reshape(b, l, d)
```

```python
import functools
import math

import jax
import jax.numpy as jnp
from jax import lax
from jax.experimental import pallas as pl
from jax.experimental.pallas import tpu as pltpu

F32 = jnp.float32
BF16 = jnp.bfloat16
EPS = 1e-6

LANES = 128
SUBLANES = 8
VMEM_LIMIT_BYTES = 56 * 1024 * 1024

RET_HEADS = 6
RET_DK = 64
RET_DV = 128
RET_QK = RET_HEADS * RET_DK
RET_V = RET_HEADS * RET_DV
RET_CHUNK = 128
ROPE_THETA = 10000.0
S5_GROUP = 16
S5_GROUPS = 16
S5_STATE = 64
S5_WIDTH = S5_GROUP * S5_GROUPS
S5_STEPS = 64
M2_HEADDIM = 64
M2_HEADS = 32
M2_GROUPS = 4
M2_HPG = M2_HEADS // M2_GROUPS
M2_STATE = 128
M2_CONV = 4
M2_CHUNK = 128
M2_DINNER = M2_HEADS * M2_HEADDIM
M2_BC = M2_GROUPS * M2_STATE
XA_HEADS = 4
N_EXPERTS = 8
ROW_TILE = 512


def _params(semantics):
    return pltpu.CompilerParams(dimension_semantics=semantics,
                                vmem_limit_bytes=VMEM_LIMIT_BYTES)


def _rms(x, w):
    return x * lax.rsqrt(jnp.mean(x * x, axis=-1, keepdims=True) + EPS) * w


def _dot(a, b):
    return jnp.dot(a, b, preferred_element_type=F32)


def _split_bf16(x):
    hi = x.astype(BF16)
    lo = (x - hi.astype(F32)).astype(BF16)
    return hi, lo


def _silu(x):
    return x * (1.0 / (1.0 + jnp.exp(-x)))


def _memkv_kernel(mem_ref, nw_ref, wk_ref, wv_ref, k_ref, v_ref):
    m = _rms(mem_ref[0], nw_ref[...]).astype(BF16)
    k_ref[0, 0] = _dot(m, wk_ref[0]).astype(BF16)
    v_ref[0, 0] = _dot(m, wv_ref[0]).astype(BF16)


def _memkv(mem, mem_norm, wk, wv):
    b, nm, d = mem.shape
    depth = wk.shape[0]
    out = jax.ShapeDtypeStruct((depth, b, nm, d), BF16)
    return pl.pallas_call(
        _memkv_kernel,
        out_shape=(out, out),
        grid=(depth, b),
        in_specs=[
            pl.BlockSpec((1, nm, d), lambda l, i: (i, 0, 0)),
            pl.BlockSpec((1, d), lambda l, i: (0, 0)),
            pl.BlockSpec((1, d, d), lambda l, i: (l, 0, 0)),
            pl.BlockSpec((1, d, d), lambda l, i: (l, 0, 0)),
        ],
        out_specs=(pl.BlockSpec((1, 1, nm, d), lambda l, i: (l, i, 0, 0)),
                   pl.BlockSpec((1, 1, nm, d), lambda l, i: (l, i, 0, 0))),
        compiler_params=_params(("arbitrary", "arbitrary")),
        name="memkv",
    )(mem, mem_norm.reshape(1, d), wk, wv)


def _in_proj_kernel(bounds, x_ref, nw_ref, w_ref, *out_refs):
    hn = _rms(x_ref[...], nw_ref[...]).astype(BF16)
    for (lo, hi), o_ref in zip(bounds, out_refs):
        o_ref[...] = _dot(hn, w_ref[:, lo:hi]).astype(o_ref.dtype)


def _in_proj(x, norm_w, w, widths, dtypes):
    n, d = x.shape
    bounds, lo = [], 0
    for wd in widths:
        bounds.append((lo, lo + wd))
        lo += wd
    assert lo == w.shape[1]
    tm = ROW_TILE
    return pl.pallas_call(
        functools.partial(_in_proj_kernel, tuple(bounds)),
        out_shape=tuple(jax.ShapeDtypeStruct((n, wd), dt) for wd, dt in zip(widths, dtypes)),
        grid=(n // tm,),
        in_specs=[
            pl.BlockSpec((tm, d), lambda i: (i, 0)),
            pl.BlockSpec((1, d), lambda i: (0, 0)),
            pl.BlockSpec(w.shape, lambda i: (0, 0)),
        ],
        out_specs=tuple(pl.BlockSpec((tm, wd), lambda i: (i, 0)) for wd in widths),
        compiler_params=_params(("arbitrary",)),
        name="in_proj",
    )(x, norm_w.reshape(1, d), w)


def _ret_kernel(q_ref, k_ref, v_ref, g_ref, pos_ref, invf_ref, rot_ref, din_ref,
                dcr_ref, dst_ref, dch_ref, o_ref, state_ref):
    @pl.when(pl.program_id(1) == 0)
    def _():
        state_ref[...] = jnp.zeros_like(state_ref)

    ang = pos_ref[0].astype(F32) * invf_ref[...]
    cos = jnp.cos(ang)
    sin = jnp.sin(ang)
    reps = RET_QK // LANES
    cos = jnp.concatenate([cos] * reps, axis=1)
    sin = jnp.concatenate([sin] * reps, axis=1)

    def rotary(x):
        hi, lo = _split_bf16(x)
        swapped = _dot(hi, rot_ref[...]) + _dot(lo, rot_ref[...])
        return x * cos + swapped * sin

    q = rotary(q_ref[0]).astype(BF16)
    kf = rotary(k_ref[0]) * (RET_DK ** -0.5)
    k = kf.astype(BF16)
    kd = (kf * dst_ref[...]).astype(BF16)
    v = v_ref[0]
    g = g_ref[0]
    state = state_ref[...]
    state_bf = state.astype(BF16)
    dcr = dcr_ref[...]
    for h in range(RET_HEADS):
        ks = slice(h * RET_DK, (h + 1) * RET_DK)
        vs = slice(h * RET_DV, (h + 1) * RET_DV)
        scores = lax.dot_general(q[:, ks], k[:, ks], (((1,), (1,)), ((), ())),
                                 preferred_element_type=F32) * din_ref[h]
        y = _dot(scores.astype(BF16), v[:, vs])
        y = y + _dot(q[:, ks], state_bf[:, vs]) * dcr[:, vs]
        upd = lax.dot_general(kd[:, ks], v[:, vs], (((0,), (0,)), ((), ())),
                              preferred_element_type=F32)
        state_ref[:, vs] = state[:, vs] * dch_ref[:, vs] + upd
        y = y * lax.rsqrt(jnp.mean(y * y, axis=-1, keepdims=True) + EPS)
        o_ref[0, :, vs] = (y * _silu(g[:, vs])).astype(o_ref.dtype)


def _retention_tables():
    t = RET_CHUNK
    log_gamma = jnp.log1p(-(2.0 ** (-5.0 - jnp.arange(RET_HEADS, dtype=F32))))
    idx = jnp.arange(t, dtype=F32)
    diff = idx[:, None] - idx[None, :]
    d_inner = jnp.where(diff[None] >= 0,
                        jnp.exp(jnp.maximum(diff, 0.0)[None] * log_gamma[:, None, None]), 0.0)
    d_cross = jnp.exp((idx[:, None] + 1.0) * log_gamma)
    d_state = jnp.exp((t - 1.0 - idx)[:, None] * log_gamma)
    d_chunk = jnp.exp(t * log_gamma)
    d_cross = jnp.repeat(d_cross, RET_DV, axis=1)
    d_state = jnp.repeat(d_state, RET_DK, axis=1)
    d_chunk = jnp.repeat(d_chunk, RET_DV)[None, :]
    half = RET_DK // 2
    inv_freq = ROPE_THETA ** (-jnp.arange(half, dtype=F32) / half)
    inv_freq = jnp.tile(inv_freq, LANES // half)[None, :]
    col = jnp.arange(RET_QK)
    first = (col % RET_DK) < half
    src = jnp.where(first, col + half, col - half)
    rot = jnp.zeros((RET_QK, RET_QK), F32).at[src, col].set(jnp.where(first, -1.0, 1.0))
    return inv_freq, rot.astype(BF16), d_inner, d_cross, d_state, d_chunk


def _retention(q, k, v, g, positions):
    b, l, _ = q.shape
    t = RET_CHUNK
    inv_freq, rot, d_inner, d_cross, d_state, d_chunk = _retention_tables()
    seq = lambda w: pl.BlockSpec((1, t, w), lambda i, c: (i, c, 0))
    const2 = lambda a: pl.BlockSpec(a.shape, lambda i, c: (0, 0))
    return pl.pallas_call(
        _ret_kernel,
        out_shape=jax.ShapeDtypeStruct((b, l, RET_V), BF16),
        grid=(b, l // t),
        in_specs=[seq(RET_QK), seq(RET_QK), seq(RET_V), seq(RET_V), seq(1),
                  const2(inv_freq), const2(rot),
                  pl.BlockSpec(d_inner.shape, lambda i, c: (0, 0, 0)),
                  const2(d_cross), const2(d_state), const2(d_chunk)],
        out_specs=seq(RET_V),
        scratch_shapes=[pltpu.VMEM((RET_DK, RET_V), F32)],
        compiler_params=_params(("arbitrary", "arbitrary")),
        name="retention",
    )(q, k, v, g, positions.reshape(b, l, 1), inv_freq, rot, d_inner, d_cross, d_state, d_chunk)


def _s5_kernel(u_ref, bin_ref, are_ref, aim_ref, cout_ref, d_ref, wg_ref, bg_ref,
               o_ref, x_scr, st_ref):
    @pl.when(pl.program_id(0) == 0)
    def _():
        st_ref[...] = jnp.zeros_like(st_ref)

    nstate = S5_GROUPS * S5_STATE
    u = u_ref[...]
    x_scr[...] = _dot(u.astype(BF16), bin_ref[...])
    rows = st_ref.shape[1]
    are = jnp.broadcast_to(are_ref[...], (rows, nstate))
    aim = jnp.broadcast_to(aim_ref[...], (rows, nstate))

    def step(t, carry):
        sr, si = carry
        r0 = pl.multiple_of(t * rows, rows)
        xr = x_scr[pl.ds(r0, rows), :nstate]
        xi = x_scr[pl.ds(r0, rows), nstate:]
        nr = are * sr - aim * si + xr
        ni = are * si + aim * sr + xi
        x_scr[pl.ds(r0, rows), :nstate] = nr
        x_scr[pl.ds(r0, rows), nstate:] = ni
        return nr, ni

    sr, si = lax.fori_loop(0, x_scr.shape[0] // rows, step, (st_ref[0], st_ref[1]))
    st_ref[0] = sr
    st_ref[1] = si
    hi, lo = _split_bf16(x_scr[...])
    y = _dot(hi, cout_ref[...]) + _dot(lo, cout_ref[...]) + d_ref[...] * u
    y = jax.nn.gelu(y)
    gate = _dot(y.astype(BF16), wg_ref[...]) + bg_ref[...]
    o_ref[...] = (y * (1.0 / (1.0 + jnp.exp(-gate)))).astype(o_ref.dtype)


def _s5_tables(lam_re, lam_im, log_dt, b_re, b_im, c_re, c_im):
    dt = jnp.exp(log_dt)[:, None]
    mag = jnp.exp(lam_re * dt)
    ab_re, ab_im = mag * jnp.cos(lam_im * dt), mag * jnp.sin(lam_im * dt)
    den = lam_re * lam_re + lam_im * lam_im
    nr, ni = ab_re - 1.0, ab_im
    f_re = (nr * lam_re + ni * lam_im) / den
    f_im = (ni * lam_re - nr * lam_im) / den
    bb_re = f_re[..., None] * b_re - f_im[..., None] * b_im
    bb_im = f_re[..., None] * b_im + f_im[..., None] * b_re
    eye = jnp.eye(S5_GROUPS, dtype=F32)
    bd = lambda m: jnp.einsum('gpc,gh->gchp', m, eye).reshape(S5_WIDTH, S5_GROUPS * S5_STATE)
    b_in = jnp.concatenate([bd(bb_re), bd(bb_im)], axis=1)
    cd = lambda m: jnp.einsum('gcp,gh->gphc', m, eye).reshape(S5_GROUPS * S5_STATE, S5_WIDTH)
    c_out = jnp.concatenate([cd(c_re), -cd(c_im)], axis=0)
    return (b_in.astype(BF16), ab_re.reshape(1, -1), ab_im.reshape(1, -1), c_out.astype(BF16))


def _s5(u_tb, batch, tables, d_skip, w_glu, b_glu):
    rows_total, width = u_tb.shape
    b_in, a_re, a_im, c_out = tables
    nstate = S5_GROUPS * S5_STATE
    tr = S5_STEPS * batch
    const = lambda a: pl.BlockSpec(a.shape, lambda i: (0, 0))
    d_skip = d_skip.reshape(1, width)
    b_glu = b_glu.reshape(1, width)
    return pl.pallas_call(
        _s5_kernel,
        out_shape=jax.ShapeDtypeStruct((rows_total, width), BF16),
        grid=(rows_total // tr,),
        in_specs=[pl.BlockSpec((tr, width), lambda i: (i, 0)),
                  const(b_in), const(a_re), const(a_im), const(c_out),
                  const(d_skip), const(w_glu), const(b_glu)],
        out_specs=pl.BlockSpec((tr, width), lambda i: (i, 0)),
        scratch_shapes=[pltpu.VMEM((tr, 2 * nstate), F32),
                        pltpu.VMEM((2, batch, nstate), F32)],
        compiler_params=_params(("arbitrary",)),
        name="s5",
    )(u_tb, b_in, a_re, a_im, c_out, d_skip, w_glu, b_glu)


def _out_xattn_kernel(n_y, *refs):
    res_ref = refs[0]
    y_refs = refs[1:1 + n_y]
    w_refs = refs[1 + n_y:1 + 2 * n_y]
    nx_ref, wq_ref, k_ref, v_ref, wo_ref, o_ref = refs[1 + 2 * n_y:]
    h = res_ref[...]
    for y_ref, w_ref in zip(y_refs, w_refs):
        h = h + _dot(y_ref[...], w_ref[...])
    d = h.shape[1]
    hd = d // XA_HEADS
    q = (_dot(_rms(h, nx_ref[...]).astype(BF16), wq_ref[...]) * (hd ** -0.5)).astype(BF16)
    k = k_ref[0]
    v = v_ref[0]
    outs = []
    for i in range(XA_HEADS):
        hs = slice(i * hd, (i + 1) * hd)
        s = lax.dot_general(q[:, hs], k[:, hs], (((1,), (1,)), ((), ())),
                            preferred_element_type=F32)
        p = jnp.exp(s - jnp.max(s, axis=-1, keepdims=True))
        p = p / jnp.sum(p, axis=-1, keepdims=True)
        outs.append(_dot(p.astype(BF16), v[:, hs]).astype(BF16))
    o = jnp.concatenate(outs, axis=1)
    o_ref[...] = h + _dot(o, wo_ref[...])


def _out_xattn(res, ys, ws, norm_w, wq, k, v, wo, seq_len):
    n, d = res.shape
    tm = ROW_TILE
    nm = k.shape[1]
    tiles_per_seq = seq_len // tm
    row = lambda w: pl.BlockSpec((tm, w), lambda i: (i, 0))
    const = lambda a: pl.BlockSpec(a.shape, lambda i: (0, 0))
    kv = pl.BlockSpec((1, nm, d), lambda i: (i // tiles_per_seq, 0, 0))
    return pl.pallas_call(
        functools.partial(_out_xattn_kernel, len(ys)),
        out_shape=jax.ShapeDtypeStruct((n, d), F32),
        grid=(n // tm,),
        in_specs=[row(d)] + [row(y.shape[1]) for y in ys] + [const(w) for w in ws]
                 + [pl.BlockSpec((1, d), lambda i: (0, 0)), const(wq), kv, kv, const(wo)],
        out_specs=row(d),
        compiler_params=_params(("arbitrary",)),
        name="out_xattn",
    )(res, *ys, *ws, norm_w.reshape(1, d), wq, k, v, wo)


def _ffn_kernel(h_ref, nw_ref, w1_ref, w3_ref, w2_ref, o_ref, hn_scr):
    f = pl.program_id(1)

    @pl.when(f == 0)
    def _():
        h = h_ref[...]
        hn_scr[...] = _rms(h, nw_ref[...]).astype(BF16)
        o_ref[...] = h

    hn = hn_scr[...]
    a = _dot(hn, w1_ref[...])
    c = _dot(hn, w3_ref[...])
    o_ref[...] += _dot((_silu(a) * c).astype(BF16), w2_ref[...])


def _ffn(h, norm_w, w1, w3, w2, n_chunks):
    n, d = h.shape
    f = w1.shape[1]
    fc = f // n_chunks
    tm = ROW_TILE
    return pl.pallas_call(
        _ffn_kernel,
        out_shape=jax.ShapeDtypeStruct((n, d), F32),
        grid=(n // tm, n_chunks),
        in_specs=[pl.BlockSpec((tm, d), lambda i, j: (i, 0)),
                  pl.BlockSpec((1, d), lambda i, j: (0, 0)),
                  pl.BlockSpec((d, fc), lambda i, j: (0, j)),
                  pl.BlockSpec((d, fc), lambda i, j: (0, j)),
                  pl.BlockSpec((fc, d), lambda i, j: (j, 0))],
        out_specs=pl.BlockSpec((tm, d), lambda i, j: (i, 0)),
        scratch_shapes=[pltpu.VMEM((tm, d), BF16)],
        compiler_params=_params(("arbitrary", "arbitrary")),
        name="ffn",
    )(h, norm_w.reshape(1, d), w1, w3, w2)


def _softplus(x):
    return jnp.maximum(x, 0.0) + jnp.log1p(jnp.exp(-jnp.abs(x)))


def _ssd_kernel(z_ref, xbc_ref, dt_ref, dtt_ref, cw_ref, cb_ref, dtb_ref, dtbt_ref,
                alog_ref, alogt_ref, dsk_ref, nw_ref, o_ref, xpad_scr, state_scr, y_scr):
    t = M2_CHUNK
    pad = SUBLANES

    @pl.when(pl.program_id(1) == 0)
    def _():
        xpad_scr[0:pad, :] = jnp.zeros((pad, xpad_scr.shape[1]), F32)
        state_scr[...] = jnp.zeros_like(state_scr)

    xbc = xbc_ref[0].astype(F32)
    xpad_scr[pad:pad + t, :] = xbc
    acc = cb_ref[...] + cw_ref[0:1, :] * xpad_scr[pl.ds(pad - M2_CONV + 1, t), :]
    for j in range(1, M2_CONV):
        acc = acc + cw_ref[j:j + 1, :] * xpad_scr[pl.ds(pad - M2_CONV + 1 + j, t), :]
    xpad_scr[0:pad, :] = xbc[t - pad:, :]
    xc = _silu(acc)
    xs = xc[:, :M2_DINNER]
    xs_bf = xs.astype(BF16)
    bm = xc[:, M2_DINNER:M2_DINNER + M2_BC].astype(BF16)
    cm = xc[:, M2_DINNER + M2_BC:]
    cm_bf = cm.astype(BF16)

    dt = _softplus(dt_ref[0] + dtb_ref[...])
    adt = dt * (-jnp.exp(alog_ref[...]))
    dtt = _softplus(dtt_ref[0] + dtbt_ref[...])
    adtt = dtt * (-jnp.exp(alogt_ref[...]))
    ri = lax.broadcasted_iota(jnp.int32, (t, t), 0)
    ci = lax.broadcasted_iota(jnp.int32, (t, t), 1)
    causal = ri >= ci
    tri = causal.astype(F32)
    acum = jnp.dot(tri, adt, preferred_element_type=F32, precision=lax.Precision.HIGHEST)
    acumt = lax.dot_general(adtt, tri, (((1,), (1,)), ((), ())), preferred_element_type=F32,
                            precision=lax.Precision.HIGHEST)
    alast = acum[t - 1:t, :]
    wdec = dt * jnp.exp(alast - acum)
    eac = jnp.exp(acum)
    sdec = jnp.exp(alast)

    state = state_scr[...]
    state_bf = state.astype(BF16)
    for g in range(M2_GROUPS):
        ns = slice(g * M2_STATE, (g + 1) * M2_STATE)
        cb = lax.dot_general(cm_bf[:, ns], bm[:, ns], (((1,), (1,)), ((), ())),
                             preferred_element_type=F32)
        wcols, scols = [], []
        for kh in range(M2_HPG):
            h = g * M2_HPG + kh
            ps = slice(h * M2_HEADDIM, (h + 1) * M2_HEADDIM)
            seg = acum[:, h:h + 1] - acumt[h:h + 1, :]
            lmat = jnp.exp(jnp.where(causal, seg, -jnp.inf))
            m = (cb * lmat * dtt[h:h + 1, :]).astype(BF16)
            coff = (cm[:, ns] * eac[:, h:h + 1]).astype(BF16)
            lhs = jnp.concatenate([m, coff], axis=1)
            rhs = jnp.concatenate([xs_bf[:, ps], state_bf[:, ps]], axis=0)
            y = _dot(lhs, rhs) + dsk_ref[:, h:h + 1] * xs[:, ps]
            y_scr[:, ps] = y
            wcols.append(jnp.broadcast_to(wdec[:, h:h + 1], (t, M2_HEADDIM)))
            scols.append(jnp.broadcast_to(sdec[:, h:h + 1], (1, M2_HEADDIM)))
        gs = slice(g * M2_HPG * M2_HEADDIM, (g + 1) * M2_HPG * M2_HEADDIM)
        xd = (xs[:, gs] * jnp.concatenate(wcols, axis=1)).astype(BF16)
        upd = lax.dot_general(bm[:, ns], xd, (((0,), (0,)), ((), ())),
                              preferred_element_type=F32)
        state_scr[:, gs] = state[:, gs] * jnp.concatenate(scols, axis=1) + upd

    yz = y_scr[...] * _silu(z_ref[0].astype(F32))
    o_ref[0] = _rms(yz, nw_ref[...]).astype(o_ref.dtype)


def _ssd(z, xbc, dt, conv_w, conv_b, dt_bias, a_log, d_skip, norm_w):
    b, l, _ = z.shape
    t = M2_CHUNK
    cdim = xbc.shape[2]
    dtt = jnp.swapaxes(dt, 1, 2)
    row = lambda a: a.reshape(1, -1)
    col = lambda a: a.reshape(-1, 1)
    seq = lambda w: pl.BlockSpec((1, t, w), lambda i, c: (i, c, 0))
    const = lambda a: pl.BlockSpec(a.shape, lambda i, c: (0, 0))
    args = [z, xbc, dt, dtt, conv_w, row(conv_b), row(dt_bias), col(dt_bias),
            row(a_log), col(a_log), row(d_skip), row(norm_w)]
    in_specs = [seq(M2_DINNER), seq(cdim), seq(M2_HEADS),
                pl.BlockSpec((1, M2_HEADS, t), lambda i, c: (i, 0, c))]
    in_specs += [const(a) for a in args[4:]]
    return pl.pallas_call(
        _ssd_kernel,
        out_shape=jax.ShapeDtypeStruct((b, l, M2_DINNER), BF16),
        grid=(b, l // t),
        in_specs=in_specs,
        out_specs=seq(M2_DINNER),
        scratch_shapes=[pltpu.VMEM((SUBLANES + t, cdim), F32),
                        pltpu.VMEM((M2_STATE, M2_DINNER), F32),
                        pltpu.VMEM((t, M2_DINNER), F32)],
        compiler_params=_params(("arbitrary", "arbitrary")),
        name="ssd",
    )(*args)


MOE_TILE = 512
GATHER_ROWS = 1024
META_I1, META_I2, META_R1, META_R2, META_G1, META_G2 = range(6)


def _router_kernel(h_ref, nw_ref, r_ref, hn_ref, meta_ref, cnt_ref, run_scr):
    @pl.when(pl.program_id(0) == 0)
    def _():
        run_scr[...] = jnp.zeros_like(run_scr)

    tm = h_ref.shape[0]
    lane = lax.broadcasted_iota(jnp.int32, (tm, LANES), 1).astype(F32)
    hn = _rms(h_ref[...], nw_ref[...])
    hn_ref[...] = hn
    logits = jnp.dot(hn, r_ref[...], preferred_element_type=F32,
                     precision=lax.Precision.HIGHEST)
    logits = jnp.where(lane < N_EXPERTS, logits, -jnp.inf)
    m1 = jnp.max(logits, axis=-1, keepdims=True)
    i1 = jnp.min(jnp.where(logits == m1, lane, float(LANES)), axis=-1, keepdims=True)
    rest = jnp.where(lane == i1, -jnp.inf, logits)
    m2 = jnp.max(rest, axis=-1, keepdims=True)
    i2 = jnp.min(jnp.where(rest == m2, lane, float(LANES)), axis=-1, keepdims=True)
    e2 = jnp.exp(m2 - m1)
    g1 = 1.0 / (1.0 + e2)
    g2 = e2 / (1.0 + e2)
    oh1 = (lane == i1).astype(F32)
    oh2 = (lane == i2).astype(F32)
    oh = oh1 + oh2
    ri = lax.broadcasted_iota(jnp.int32, (tm, tm), 0)
    ci = lax.broadcasted_iota(jnp.int32, (tm, tm), 1)
    before = _dot((ri > ci).astype(BF16), oh.astype(BF16)) + run_scr[0:1, :]
    r1 = jnp.sum(oh1 * before, axis=-1, keepdims=True)
    r2 = jnp.sum(oh2 * before, axis=-1, keepdims=True)
    run_scr[...] = run_scr[...] + jnp.sum(oh, axis=0, keepdims=True)
    cnt_ref[...] = run_scr[...]
    meta = jnp.zeros((tm, LANES), F32)
    for col, val in ((META_I1, i1), (META_I2, i2), (META_R1, r1), (META_R2, r2),
                     (META_G1, g1), (META_G2, g2)):
        meta = jnp.where(lane == float(col), val, meta)
    meta_ref[...] = meta


def _router(h, norm_w, router):
    n, d = h.shape
    tm = ROW_TILE
    router_p = jnp.zeros((d, LANES), F32).at[:, :router.shape[1]].set(router)
    return pl.pallas_call(
        _router_kernel,
        out_shape=(jax.ShapeDtypeStruct((n, d), F32),
                   jax.ShapeDtypeStruct((n, LANES), F32),
                   jax.ShapeDtypeStruct((SUBLANES, LANES), F32)),
        grid=(n // tm,),
        in_specs=[pl.BlockSpec((tm, d), lambda i: (i, 0)),
                  pl.BlockSpec((1, d), lambda i: (0, 0)),
                  pl.BlockSpec((d, LANES), lambda i: (0, 0))],
        out_specs=(pl.BlockSpec((tm, d), lambda i: (i, 0)),
                   pl.BlockSpec((tm, LANES), lambda i: (i, 0)),
                   pl.BlockSpec((SUBLANES, LANES), lambda i: (0, 0))),
        scratch_shapes=[pltpu.VMEM((SUBLANES, LANES), F32)],
        compiler_params=_params(("arbitrary",)),
        name="router",
    )(h, norm_w.reshape(1, d), router_p)


def _row_copy(src_hbm, src_row, dst_ref, dst_row, sem):
    return pltpu.make_async_copy(src_hbm.at[pl.ds(src_row, 1)], dst_ref.at[pl.ds(dst_row, 1)], sem)


def _gather_kernel(pos1_ref, pos2_ref, hn_hbm, xs_in_hbm, xs_hbm, sems):
    del xs_in_hbm
    rows = pos1_ref.shape[2]
    base = pl.program_id(0) * rows

    def issue(t, carry):
        _row_copy(hn_hbm, base + t, xs_hbm, pos1_ref[0, 0, t], sems.at[0]).start()
        _row_copy(hn_hbm, base + t, xs_hbm, pos2_ref[0, 0, t], sems.at[1]).start()
        return carry

    lax.fori_loop(0, rows, issue, 0, unroll=8)

    def drain(t, carry):
        _row_copy(hn_hbm, 0, xs_hbm, 0, sems.at[0]).wait()
        _row_copy(hn_hbm, 0, xs_hbm, 0, sems.at[1]).wait()
        return carry

    lax.fori_loop(0, rows, drain, 0, unroll=8)


def _gather_rows(hn, pos1, pos2, n_sorted):
    n, d = hn.shape
    rows = GATHER_ROWS
    steps = n // rows
    idx = lambda: pl.BlockSpec((1, 1, rows), lambda i: (i, 0, 0), memory_space=pltpu.SMEM)
    return pl.pallas_call(
        _gather_kernel,
        out_shape=jax.ShapeDtypeStruct((n_sorted, d), F32),
        grid=(steps,),
        in_specs=[idx(), idx(),
                  pl.BlockSpec(memory_space=pl.ANY), pl.BlockSpec(memory_space=pl.ANY)],
        out_specs=pl.BlockSpec(memory_space=pl.ANY),
        scratch_shapes=[pltpu.SemaphoreType.DMA((2,))],
        input_output_aliases={3: 0},
        compiler_params=_params(("arbitrary",)),
        name="moe_gather",
    )(pos1.reshape(steps, 1, rows), pos2.reshape(steps, 1, rows), hn,
      jnp.zeros((n_sorted, d), F32))


def _gmm_kernel(te_ref, tv_ref, x_ref, w1_ref, w3_ref, w2_ref, o_ref, xb_scr):
    del te_ref
    f = pl.program_id(1)

    @pl.when(f == 0)
    def _():
        xb_scr[...] = x_ref[...].astype(BF16)
        o_ref[...] = jnp.zeros_like(o_ref)

    @pl.when(tv_ref[pl.program_id(0)] != 0)
    def _():
        x = xb_scr[...]
        a = _dot(x, w1_ref[0])
        c = _dot(x, w3_ref[0])
        o_ref[...] += _dot((_silu(a) * c).astype(BF16), w2_ref[0])


def _grouped_swiglu(xs, tile_expert, tile_valid, w1, w3, w2, n_chunks):
    p, d = xs.shape
    fc = w1.shape[2] // n_chunks
    tg = MOE_TILE
    grid_spec = pltpu.PrefetchScalarGridSpec(
        num_scalar_prefetch=2,
        grid=(p // tg, n_chunks),
        in_specs=[pl.BlockSpec((tg, d), lambda j, f, te, tv: (j, 0)),
                  pl.BlockSpec((1, d, fc), lambda j, f, te, tv: (te[j], 0, f)),
                  pl.BlockSpec((1, d, fc), lambda j, f, te, tv: (te[j], 0, f)),
                  pl.BlockSpec((1, fc, d), lambda j, f, te, tv: (te[j], f, 0))],
        out_specs=pl.BlockSpec((tg, d), lambda j, f, te, tv: (j, 0)),
        scratch_shapes=[pltpu.VMEM((tg, d), BF16)],
    )
    return pl.pallas_call(
        _gmm_kernel,
        out_shape=jax.ShapeDtypeStruct((p, d), F32),
        grid_spec=grid_spec,
        compiler_params=_params(("arbitrary", "arbitrary")),
        name="moe_gmm",
    )(tile_expert, tile_valid, xs, w1, w3, w2)


def _combine_kernel(pos1_ref, pos2_ref, h_ref, meta_ref, fw_ref, ys_hbm, o_ref,
                    y1_scr, y2_scr, sems):
    rows = h_ref.shape[0]

    def issue(t, carry):
        _row_copy(ys_hbm, pos1_ref[0, 0, t], y1_scr, t, sems.at[0]).start()
        _row_copy(ys_hbm, pos2_ref[0, 0, t], y2_scr, t, sems.at[1]).start()
        return carry

    lax.fori_loop(0, rows, issue, 0, unroll=8)

    def drain(t, carry):
        _row_copy(ys_hbm, 0, y1_scr, 0, sems.at[0]).wait()
        _row_copy(ys_hbm, 0, y2_scr, 0, sems.at[1]).wait()
        return carry

    lax.fori_loop(0, rows, drain, 0, unroll=8)
    meta = meta_ref[...]
    g1 = meta[:, META_G1:META_G1 + 1]
    g2 = meta[:, META_G2:META_G2 + 1]
    o_ref[...] = _rms(h_ref[...] + g1 * y1_scr[...] + g2 * y2_scr[...], fw_ref[...])


def _combine(h, meta, pos1, pos2, ys, final_w):
    n, d = h.shape
    tm = ROW_TILE
    steps = n // tm
    idx = lambda: pl.BlockSpec((1, 1, tm), lambda i: (i, 0, 0), memory_space=pltpu.SMEM)
    return pl.pallas_call(
        _combine_kernel,
        out_shape=jax.ShapeDtypeStruct((n, d), F32),
        grid=(steps,),
        in_specs=[idx(), idx(),
                  pl.BlockSpec((tm, d), lambda i: (i, 0)),
                  pl.BlockSpec((tm, LANES), lambda i: (i, 0)),
                  pl.BlockSpec((1, d), lambda i: (0, 0)),
                  pl.BlockSpec(memory_space=pl.ANY)],
        out_specs=pl.BlockSpec((tm, d), lambda i: (i, 0)),
        scratch_shapes=[pltpu.VMEM((tm, d), F32), pltpu.VMEM((tm, d), F32),
                        pltpu.SemaphoreType.DMA((2,))],
        compiler_params=_params(("arbitrary",)),
        name="moe_combine",
    )(pos1.reshape(steps, 1, tm), pos2.reshape(steps, 1, tm), h, meta,
      final_w.reshape(1, d), ys)


def _moe(h, norm_w, router, w1, w3, w2, final_w, n_chunks):
    n, d = h.shape
    ne = w1.shape[0]
    tg = MOE_TILE
    hn, meta, counts = _router(h, norm_w, router)
    counts = counts[0, :ne].astype(jnp.int32)
    padded = ((counts + tg - 1) // tg) * tg
    ends = jnp.cumsum(padded)
    offs = ends - padded
    col = lambda c: meta[:, c].astype(jnp.int32)
    pos1 = offs[col(META_I1)] + col(META_R1)
    pos2 = offs[col(META_I2)] + col(META_R2)
    n_tiles = (2 * n) // tg + ne
    starts = jnp.arange(n_tiles, dtype=jnp.int32) * tg
    tile_expert = jnp.minimum(jnp.searchsorted(ends, starts, side='right'), ne - 1).astype(jnp.int32)
    tile_valid = (starts < ends[-1]).astype(jnp.int32)
    xs = _gather_rows(hn, pos1, pos2, n_tiles * tg)
    ys = _grouped_swiglu(xs, tile_expert, tile_valid, w1, w3, w2, n_chunks)
    return _combine(h, meta, pos1, pos2, ys, final_w)


def kernel(x, mem, positions, mem_norm, norm_mix, norm_xattn, norm_ffn, xa_wq, xa_wk, xa_wv, xa_wo, ev_w_in, ev_s5_lam_re, ev_s5_lam_im, ev_s5_log_dt, ev_s5_b_re, ev_s5_b_im, ev_s5_c_re, ev_s5_c_im, ev_s5_d, ev_s5_w_glu, ev_s5_b_glu, ev_w_out, ev_ffn_w1, ev_ffn_w3, ev_ffn_w2, od_w_in, od_conv_w, od_conv_b, od_dt_bias, od_a_log, od_d, od_norm, od_w_out, od_router, od_moe_w1, od_moe_w3, od_moe_w2, final_norm):
    b, l, d = x.shape
    n = b * l
    assert l % ROW_TILE == 0 and l % RET_CHUNK == 0 and l % M2_CHUNK == 0 and l % S5_STEPS == 0
    assert b == SUBLANES, "the S5 scan maps the batch onto the sublanes"
    bf = lambda a: a.astype(BF16)

    mem_k, mem_v = _memkv(mem, mem_norm, bf(xa_wk), bf(xa_wv))
    h = x.reshape(n, d)

    q, k, v, g, u = _in_proj(
        h, norm_mix[0], bf(ev_w_in[0]),
        (RET_QK, RET_QK, RET_V, RET_V, S5_WIDTH), (F32, F32, BF16, F32, F32))
    y_ret = _retention(q.reshape(b, l, -1), k.reshape(b, l, -1), v.reshape(b, l, -1),
                       g.reshape(b, l, -1), positions)
    u_tb = jnp.swapaxes(u.reshape(b, l, S5_WIDTH), 0, 1).reshape(n, S5_WIDTH)
    tables = _s5_tables(ev_s5_lam_re[0], ev_s5_lam_im[0], ev_s5_log_dt[0], ev_s5_b_re[0],
                        ev_s5_b_im[0], ev_s5_c_re[0], ev_s5_c_im[0])
    y_s5 = _s5(u_tb, b, tables, ev_s5_d[0], bf(ev_s5_w_glu[0]), ev_s5_b_glu[0])
    y_s5 = jnp.swapaxes(y_s5.reshape(l, b, S5_WIDTH), 0, 1).reshape(n, S5_WIDTH)
    w_out = bf(ev_w_out[0])
    h = _out_xattn(h, [y_ret.reshape(n, RET_V), y_s5], [w_out[:RET_V], w_out[RET_V:]],
                   norm_xattn[0], bf(xa_wq[0]), mem_k[0], mem_v[0], bf(xa_wo[0]), l)
    h = _ffn(h, norm_ffn[0], bf(ev_ffn_w1[0]), bf(ev_ffn_w3[0]), bf(ev_ffn_w2[0]), 2)

    cdim = od_conv_w.shape[2]
    z, xbc, dt = _in_proj(h, norm_mix[1], bf(od_w_in[0]),
                          (M2_DINNER, cdim, M2_HEADS), (BF16, BF16, F32))
    y = _ssd(z.reshape(b, l, -1), xbc.reshape(b, l, -1), dt.reshape(b, l, -1),
             od_conv_w[0], od_conv_b[0], od_dt_bias[0], od_a_log[0], od_d[0], od_norm[0])
    h = _out_xattn(h, [y.reshape(n, M2_DINNER)], [bf(od_w_out[0])],
                   norm_xattn[1], bf(xa_wq[1]), mem_k[1], mem_v[1], bf(xa_wo[1]), l)
    out = _moe(h, norm_ffn[1], od_router[0], bf(od_moe_w1[0]), bf(od_moe_w3[0]),
               bf(od_moe_w2[0]), final_norm, 2)
    return out.reshape(b, l, d)
```

```python
import functools
import math

import jax
import jax.numpy as jnp
from jax import lax
from jax.experimental import pallas as pl
from jax.experimental.pallas import tpu as pltpu

F32 = jnp.float32
BF16 = jnp.bfloat16
EPS = 1e-6

LANES = 128
SUBLANES = 8
VMEM_LIMIT_BYTES = 56 * 1024 * 1024

RET_HEADS = 6
RET_DK = 64
RET_DV = 128
RET_QK = RET_HEADS * RET_DK
RET_V = RET_HEADS * RET_DV
RET_CHUNK = 128
ROPE_THETA = 10000.0
S5_GROUP = 16
S5_GROUPS = 16
S5_STATE = 64
S5_WIDTH = S5_GROUP * S5_GROUPS
S5_STEPS = 64
M2_HEADDIM = 64
M2_HEADS = 32
M2_GROUPS = 4
M2_HPG = M2_HEADS // M2_GROUPS
M2_STATE = 128
M2_CONV = 4
M2_CHUNK = 128
M2_DINNER = M2_HEADS * M2_HEADDIM
M2_BC = M2_GROUPS * M2_STATE
XA_HEADS = 4
N_EXPERTS = 8
ROW_TILE = 512


def _params(semantics):
    return pltpu.CompilerParams(dimension_semantics=semantics,
                                vmem_limit_bytes=VMEM_LIMIT_BYTES)


def _rms(x, w):
    return x * lax.rsqrt(jnp.mean(x * x, axis=-1, keepdims=True) + EPS) * w


def _dot(a, b):
    return jnp.dot(a, b, preferred_element_type=F32)


def _split_bf16(x):
    hi = x.astype(BF16)
    lo = (x - hi.astype(F32)).astype(BF16)
    return hi, lo


def _silu(x):
    h = 0.5 * x
    return h + h * jnp.tanh(h)


def _memkv_kernel(mem_ref, nw_ref, wk_ref, wv_ref, k_ref, v_ref):
    m = _rms(mem_ref[0], nw_ref[...]).astype(BF16)
    k_ref[0, 0] = _dot(m, wk_ref[0]).astype(BF16)
    v_ref[0, 0] = _dot(m, wv_ref[0]).astype(BF16)


def _memkv(mem, mem_norm, wk, wv):
    b, nm, d = mem.shape
    depth = wk.shape[0]
    out = jax.ShapeDtypeStruct((depth, b, nm, d), BF16)
    return pl.pallas_call(
        _memkv_kernel,
        out_shape=(out, out),
        grid=(depth, b),
        in_specs=[
            pl.BlockSpec((1, nm, d), lambda l, i: (i, 0, 0)),
            pl.BlockSpec((1, d), lambda l, i: (0, 0)),
            pl.BlockSpec((1, d, d), lambda l, i: (l, 0, 0)),
            pl.BlockSpec((1, d, d), lambda l, i: (l, 0, 0)),
        ],
        out_specs=(pl.BlockSpec((1, 1, nm, d), lambda l, i: (l, i, 0, 0)),
                   pl.BlockSpec((1, 1, nm, d), lambda l, i: (l, i, 0, 0))),
        compiler_params=_params(("arbitrary", "arbitrary")),
        name="memkv",
    )(mem, mem_norm.reshape(1, d), wk, wv)


def _in_proj_kernel(bounds, x_ref, nw_ref, w_ref, *out_refs):
    hn = _rms(x_ref[...], nw_ref[...]).astype(BF16)
    for (lo, hi), o_ref in zip(bounds, out_refs):
        o_ref[...] = _dot(hn, w_ref[:, lo:hi]).astype(o_ref.dtype)


def _in_proj(x, norm_w, w, widths, dtypes):
    n, d = x.shape
    bounds, lo = [], 0
    for wd in widths:
        bounds.append((lo, lo + wd))
        lo += wd
    assert lo == w.shape[1]
    tm = ROW_TILE
    return pl.pallas_call(
        functools.partial(_in_proj_kernel, tuple(bounds)),
        out_shape=tuple(jax.ShapeDtypeStruct((n, wd), dt) for wd, dt in zip(widths, dtypes)),
        grid=(n // tm,),
        in_specs=[
            pl.BlockSpec((tm, d), lambda i: (i, 0)),
            pl.BlockSpec((1, d), lambda i: (0, 0)),
            pl.BlockSpec(w.shape, lambda i: (0, 0)),
        ],
        out_specs=tuple(pl.BlockSpec((tm, wd), lambda i: (i, 0)) for wd in widths),
        compiler_params=_params(("arbitrary",)),
        name="in_proj",
    )(x, norm_w.reshape(1, d), w)


def _ret_kernel(q_ref, k_ref, v_ref, g_ref, pos_ref, invf_ref, rot_ref, din_ref,
                dcr_ref, dst_ref, dch_ref, o_ref, state_ref):
    @pl.when(pl.program_id(1) == 0)
    def _():
        state_ref[...] = jnp.zeros_like(state_ref)

    ang = pos_ref[0].astype(F32) * invf_ref[...]
    cos = jnp.cos(ang)
    sin = jnp.sin(ang)
    reps = RET_QK // LANES
    cos = jnp.concatenate([cos] * reps, axis=1)
    sin = jnp.concatenate([sin] * reps, axis=1)

    def rotary(x):
        hi, lo = _split_bf16(x)
        swapped = _dot(hi, rot_ref[...]) + _dot(lo, rot_ref[...])
        return x * cos + swapped * sin

    q = rotary(q_ref[0]).astype(BF16)
    kf = rotary(k_ref[0]) * (RET_DK ** -0.5)
    k = kf.astype(BF16)
    kd = (kf * dst_ref[...]).astype(BF16)
    v = v_ref[0]
    g = g_ref[0]
    state = state_ref[...]
    state_bf = state.astype(BF16)
    dcr = dcr_ref[...]
    for h in range(RET_HEADS):
        ks = slice(h * RET_DK, (h + 1) * RET_DK)
        vs = slice(h * RET_DV, (h + 1) * RET_DV)
        scores = lax.dot_general(q[:, ks], k[:, ks], (((1,), (1,)), ((), ())),
                                 preferred_element_type=F32) * din_ref[h]
        y = _dot(scores.astype(BF16), v[:, vs])
        y = y + _dot(q[:, ks], state_bf[:, vs]) * dcr[:, vs]
        upd = lax.dot_general(kd[:, ks], v[:, vs], (((0,), (0,)), ((), ())),
                              preferred_element_type=F32)
        state_ref[:, vs] = state[:, vs] * dch_ref[:, vs] + upd
        y = y * lax.rsqrt(jnp.mean(y * y, axis=-1, keepdims=True) + EPS)
        o_ref[0, :, vs] = (y * _silu(g[:, vs])).astype(o_ref.dtype)


def _retention_tables():
    t = RET_CHUNK
    log_gamma = jnp.log1p(-(2.0 ** (-5.0 - jnp.arange(RET_HEADS, dtype=F32))))
    idx = jnp.arange(t, dtype=F32)
    diff = idx[:, None] - idx[None, :]
    d_inner = jnp.where(diff[None] >= 0,
                        jnp.exp(jnp.maximum(diff, 0.0)[None] * log_gamma[:, None, None]), 0.0)
    d_cross = jnp.exp((idx[:, None] + 1.0) * log_gamma)
    d_state = jnp.exp((t - 1.0 - idx)[:, None] * log_gamma)
    d_chunk = jnp.exp(t * log_gamma)
    d_cross = jnp.repeat(d_cross, RET_DV, axis=1)
    d_state = jnp.repeat(d_state, RET_DK, axis=1)
    d_chunk = jnp.repeat(d_chunk, RET_DV)[None, :]
    half = RET_DK // 2
    inv_freq = ROPE_THETA ** (-jnp.arange(half, dtype=F32) / half)
    inv_freq = jnp.tile(inv_freq, LANES // half)[None, :]
    col = jnp.arange(RET_QK)
    first = (col % RET_DK) < half
    src = jnp.where(first, col + half, col - half)
    rot = jnp.zeros((RET_QK, RET_QK), F32).at[src, col].set(jnp.where(first, -1.0, 1.0))
    return inv_freq, rot.astype(BF16), d_inner, d_cross, d_state, d_chunk


def _retention(q, k, v, g, positions):
    b, l, _ = q.shape
    t = RET_CHUNK
    inv_freq, rot, d_inner, d_cross, d_state, d_chunk = _retention_tables()
    seq = lambda w: pl.BlockSpec((1, t, w), lambda i, c: (i, c, 0))
    const2 = lambda a: pl.BlockSpec(a.shape, lambda i, c: (0, 0))
    return pl.pallas_call(
        _ret_kernel,
        out_shape=jax.ShapeDtypeStruct((b, l, RET_V), BF16),
        grid=(b, l // t),
        in_specs=[seq(RET_QK), seq(RET_QK), seq(RET_V), seq(RET_V), seq(1),
                  const2(inv_freq), const2(rot),
                  pl.BlockSpec(d_inner.shape, lambda i, c: (0, 0, 0)),
                  const2(d_cross), const2(d_state), const2(d_chunk)],
        out_specs=seq(RET_V),
        scratch_shapes=[pltpu.VMEM((RET_DK, RET_V), F32)],
        compiler_params=_params(("arbitrary", "arbitrary")),
        name="retention",
    )(q, k, v, g, positions.reshape(b, l, 1), inv_freq, rot, d_inner, d_cross, d_state, d_chunk)


def _s5_kernel(u_ref, bin_ref, are_ref, aim_ref, cout_ref, d_ref, wg_ref, bg_ref,
               o_ref, x_scr, st_ref):
    @pl.when(pl.program_id(0) == 0)
    def _():
        st_ref[...] = jnp.zeros_like(st_ref)

    nstate = S5_GROUPS * S5_STATE
    u = u_ref[...]
    x_scr[...] = _dot(u.astype(BF16), bin_ref[...])
    rows = st_ref.shape[1]
    are = jnp.broadcast_to(are_ref[...], (rows, nstate))
    aim = jnp.broadcast_to(aim_ref[...], (rows, nstate))

    def step(t, carry):
        sr, si = carry
        r0 = pl.multiple_of(t * rows, rows)
        xr = x_scr[pl.ds(r0, rows), :nstate]
        xi = x_scr[pl.ds(r0, rows), nstate:]
        nr = are * sr - aim * si + xr
        ni = are * si + aim * sr + xi
        x_scr[pl.ds(r0, rows), :nstate] = nr
        x_scr[pl.ds(r0, rows), nstate:] = ni
        return nr, ni

    sr, si = lax.fori_loop(0, x_scr.shape[0] // rows, step, (st_ref[0], st_ref[1]))
    st_ref[0] = sr
    st_ref[1] = si
    hi, lo = _split_bf16(x_scr[...])
    y = _dot(hi, cout_ref[...]) + _dot(lo, cout_ref[...]) + d_ref[...] * u
    y = jax.nn.gelu(y)
    gate = _dot(y.astype(BF16), wg_ref[...]) + bg_ref[...]
    o_ref[...] = (y * (1.0 / (1.0 + jnp.exp(-gate)))).astype(o_ref.dtype)


def _s5_tables(lam_re, lam_im, log_dt, b_re, b_im, c_re, c_im):
    dt = jnp.exp(log_dt)[:, None]
    mag = jnp.exp(lam_re * dt)
    ab_re, ab_im = mag * jnp.cos(lam_im * dt), mag * jnp.sin(lam_im * dt)
    den = lam_re * lam_re + lam_im * lam_im
    nr, ni = ab_re - 1.0, ab_im
    f_re = (nr * lam_re + ni * lam_im) / den
    f_im = (ni * lam_re - nr * lam_im) / den
    bb_re = f_re[..., None] * b_re - f_im[..., None] * b_im
    bb_im = f_re[..., None] * b_im + f_im[..., None] * b_re
    eye = jnp.eye(S5_GROUPS, dtype=F32)
    bd = lambda m: jnp.einsum('gpc,gh->gchp', m, eye).reshape(S5_WIDTH, S5_GROUPS * S5_STATE)
    b_in = jnp.concatenate([bd(bb_re), bd(bb_im)], axis=1)
    cd = lambda m: jnp.einsum('gcp,gh->gphc', m, eye).reshape(S5_GROUPS * S5_STATE, S5_WIDTH)
    c_out = jnp.concatenate([cd(c_re), -cd(c_im)], axis=0)
    return (b_in.astype(BF16), ab_re.reshape(1, -1), ab_im.reshape(1, -1), c_out.astype(BF16))


def _s5(u_tb, batch, tables, d_skip, w_glu, b_glu):
    rows_total, width = u_tb.shape
    b_in, a_re, a_im, c_out = tables
    nstate = S5_GROUPS * S5_STATE
    tr = S5_STEPS * batch
    const = lambda a: pl.BlockSpec(a.shape, lambda i: (0, 0))
    d_skip = d_skip.reshape(1, width)
    b_glu = b_glu.reshape(1, width)
    return pl.pallas_call(
        _s5_kernel,
        out_shape=jax.ShapeDtypeStruct((rows_total, width), BF16),
        grid=(rows_total // tr,),
        in_specs=[pl.BlockSpec((tr, width), lambda i: (i, 0)),
                  const(b_in), const(a_re), const(a_im), const(c_out),
                  const(d_skip), const(w_glu), const(b_glu)],
        out_specs=pl.BlockSpec((tr, width), lambda i: (i, 0)),
        scratch_shapes=[pltpu.VMEM((tr, 2 * nstate), F32),
                        pltpu.VMEM((2, batch, nstate), F32)],
        compiler_params=_params(("arbitrary",)),
        name="s5",
    )(u_tb, b_in, a_re, a_im, c_out, d_skip, w_glu, b_glu)


def _out_xattn_kernel(n_y, *refs):
    res_ref = refs[0]
    y_refs = refs[1:1 + n_y]
    w_refs = refs[1 + n_y:1 + 2 * n_y]
    nx_ref, wq_ref, k_ref, v_ref, wo_ref, o_ref = refs[1 + 2 * n_y:]
    h = res_ref[...]
    for y_ref, w_ref in zip(y_refs, w_refs):
        h = h + _dot(y_ref[...], w_ref[...])
    d = h.shape[1]
    hd = d // XA_HEADS
    q = (_dot(_rms(h, nx_ref[...]).astype(BF16), wq_ref[...]) * (hd ** -0.5)).astype(BF16)
    k = k_ref[0]
    v = v_ref[0]
    outs = []
    for i in range(XA_HEADS):
        hs = slice(i * hd, (i + 1) * hd)
        s = lax.dot_general(q[:, hs], k[:, hs], (((1,), (1,)), ((), ())),
                            preferred_element_type=F32)
        p = jnp.exp(s - jnp.max(s, axis=-1, keepdims=True))
        p = p / jnp.sum(p, axis=-1, keepdims=True)
        outs.append(_dot(p.astype(BF16), v[:, hs]).astype(BF16))
    o = jnp.concatenate(outs, axis=1)
    o_ref[...] = h + _dot(o, wo_ref[...])


def _out_xattn(res, ys, ws, norm_w, wq, k, v, wo, seq_len):
    n, d = res.shape
    tm = ROW_TILE
    nm = k.shape[1]
    tiles_per_seq = seq_len // tm
    row = lambda w: pl.BlockSpec((tm, w), lambda i: (i, 0))
    const = lambda a: pl.BlockSpec(a.shape, lambda i: (0, 0))
    kv = pl.BlockSpec((1, nm, d), lambda i: (i // tiles_per_seq, 0, 0))
    return pl.pallas_call(
        functools.partial(_out_xattn_kernel, len(ys)),
        out_shape=jax.ShapeDtypeStruct((n, d), F32),
        grid=(n // tm,),
        in_specs=[row(d)] + [row(y.shape[1]) for y in ys] + [const(w) for w in ws]
                 + [pl.BlockSpec((1, d), lambda i: (0, 0)), const(wq), kv, kv, const(wo)],
        out_specs=row(d),
        compiler_params=_params(("arbitrary",)),
        name="out_xattn",
    )(res, *ys, *ws, norm_w.reshape(1, d), wq, k, v, wo)


def _ffn_kernel(h_ref, nw_ref, w1_ref, w3_ref, w2_ref, o_ref, hn_scr):
    f = pl.program_id(1)

    @pl.when(f == 0)
    def _():
        h = h_ref[...]
        hn_scr[...] = _rms(h, nw_ref[...]).astype(BF16)
        o_ref[...] = h

    hn = hn_scr[...]
    a = _dot(hn, w1_ref[...])
    c = _dot(hn, w3_ref[...])
    o_ref[...] += _dot((_silu(a) * c).astype(BF16), w2_ref[...])


def _ffn(h, norm_w, w1, w3, w2, n_chunks):
    n, d = h.shape
    f = w1.shape[1]
    fc = f // n_chunks
    tm = ROW_TILE
    return pl.pallas_call(
        _ffn_kernel,
        out_shape=jax.ShapeDtypeStruct((n, d), F32),
        grid=(n // tm, n_chunks),
        in_specs=[pl.BlockSpec((tm, d), lambda i, j: (i, 0)),
                  pl.BlockSpec((1, d), lambda i, j: (0, 0)),
                  pl.BlockSpec((d, fc), lambda i, j: (0, j)),
                  pl.BlockSpec((d, fc), lambda i, j: (0, j)),
                  pl.BlockSpec((fc, d), lambda i, j: (j, 0))],
        out_specs=pl.BlockSpec((tm, d), lambda i, j: (i, 0)),
        scratch_shapes=[pltpu.VMEM((tm, d), BF16)],
        compiler_params=_params(("arbitrary", "arbitrary")),
        name="ffn",
    )(h, norm_w.reshape(1, d), w1, w3, w2)


def _softplus(x):
    return jnp.maximum(x, 0.0) + jnp.log1p(jnp.exp(-jnp.abs(x)))


def _m2_in_kernel(tiles_per_seq, x_ref, nw_ref, w_ref, cw_ref, cb_ref, zs_ref, xc_ref, dt_ref,
                  xpad_scr):
    tm = x_ref.shape[0]
    pad = SUBLANES
    cdim = xc_ref.shape[1]
    hn = _rms(x_ref[...], nw_ref[...]).astype(BF16)
    zs_ref[...] = _silu(_dot(hn, w_ref[:, :M2_DINNER])).astype(zs_ref.dtype)
    dt_ref[...] = _dot(hn, w_ref[:, M2_DINNER + cdim:])

    @pl.when(pl.program_id(0) % tiles_per_seq == 0)
    def _():
        xpad_scr[0:pad, :] = jnp.zeros((pad, cdim), F32)

    blk = 4 * LANES
    for c0 in range(0, cdim, blk):
        cs = slice(c0, c0 + blk)
        xbc = _dot(hn, w_ref[:, M2_DINNER + c0:M2_DINNER + c0 + blk])
        xpad_scr[pad:pad + tm, cs] = xbc
        ext = xpad_scr[:, cs]
        acc = cb_ref[:, cs] + cw_ref[M2_CONV - 1:M2_CONV, cs] * xbc
        for j in range(M2_CONV - 1):
            shifted = pltpu.roll(ext, M2_CONV - 1 - j, 0)[pad:pad + tm, :]
            acc = acc + cw_ref[j:j + 1, cs] * shifted
        xc_ref[:, cs] = _silu(acc).astype(xc_ref.dtype)
        xpad_scr[0:pad, cs] = xbc[tm - pad:, :]


def _m2_in_proj(x, norm_w, w, conv_w, conv_b, seq_len):
    n, d = x.shape
    cdim = conv_w.shape[1]
    tm = ROW_TILE
    row = lambda wd: pl.BlockSpec((tm, wd), lambda i: (i, 0))
    const = lambda a: pl.BlockSpec(a.shape, lambda i: (0, 0))
    conv_b = conv_b.reshape(1, cdim)
    return pl.pallas_call(
        functools.partial(_m2_in_kernel, seq_len // tm),
        out_shape=(jax.ShapeDtypeStruct((n, M2_DINNER), BF16),
                   jax.ShapeDtypeStruct((n, cdim), BF16),
                   jax.ShapeDtypeStruct((n, M2_HEADS), F32)),
        grid=(n // tm,),
        in_specs=[row(d), pl.BlockSpec((1, d), lambda i: (0, 0)), const(w), const(conv_w),
                  const(conv_b)],
        out_specs=(row(M2_DINNER), row(cdim), row(M2_HEADS)),
        scratch_shapes=[pltpu.VMEM((SUBLANES + tm, cdim), F32)],
        compiler_params=_params(("arbitrary",)),
        name="m2_in_proj",
    )(x, norm_w.reshape(1, d), w, conv_w, conv_b)


def _ssd_kernel(zs_ref, xc_ref, dt_ref, dtt_ref, dtb_ref, dtbt_ref, alog_ref, alogt_ref,
                dsk_ref, nw_ref, o_ref, state_scr, y_scr):
    t = M2_CHUNK

    @pl.when(pl.program_id(1) == 0)
    def _():
        state_scr[...] = jnp.zeros_like(state_scr)

    xc = xc_ref[0]
    xs_bf = xc[:, :M2_DINNER]
    bm = xc[:, M2_DINNER:M2_DINNER + M2_BC]
    cm = xc[:, M2_DINNER + M2_BC:]

    dt = _softplus(dt_ref[0] + dtb_ref[...])
    adt = dt * (-jnp.exp(alog_ref[...]))
    dtt = _softplus(dtt_ref[0] + dtbt_ref[...])
    adtt = dtt * (-jnp.exp(alogt_ref[...]))
    ri = lax.broadcasted_iota(jnp.int32, (t, t), 0)
    ci = lax.broadcasted_iota(jnp.int32, (t, t), 1)
    causal = ri >= ci
    tri = causal.astype(F32)
    acum = jnp.dot(tri, adt, preferred_element_type=F32, precision=lax.Precision.HIGHEST)
    acumt = lax.dot_general(adtt, tri, (((1,), (1,)), ((), ())), preferred_element_type=F32,
                            precision=lax.Precision.HIGHEST)
    arow = acumt - jnp.log(dtt)
    alast = acum[t - 1:t, :]
    wdec = (dt * jnp.exp(alast - acum)).astype(BF16)
    eac = jnp.exp(acum).astype(BF16)
    sdec = jnp.exp(alast)

    state = state_scr[...]
    state_bf = state.astype(BF16)
    lane = lax.broadcasted_iota(jnp.int32, (t, M2_DINNER), 1)
    even = (lane % (2 * M2_HEADDIM)) < M2_HEADDIM
    zero = jnp.zeros((), BF16)
    rhs_half = (jnp.concatenate([jnp.where(even, xs_bf, zero), jnp.where(even, state_bf, zero)], axis=0),
                jnp.concatenate([jnp.where(even, zero, xs_bf), jnp.where(even, zero, state_bf)], axis=0))
    for g in range(M2_GROUPS):
        ns = slice(g * M2_STATE, (g + 1) * M2_STATE)
        cb = lax.dot_general(cm[:, ns], bm[:, ns], (((1,), (1,)), ((), ())),
                             preferred_element_type=F32).astype(BF16)
        wcols, scols = [], []
        for kp in range(M2_HPG // 2):
            pair = slice((g * M2_HPG + 2 * kp) * M2_HEADDIM, (g * M2_HPG + 2 * kp + 2) * M2_HEADDIM)
            y = None
            for half in range(2):
                h = g * M2_HPG + 2 * kp + half
                seg = acum[:, h:h + 1] - arow[h:h + 1, :]
                lmat = jnp.exp(jnp.where(causal, seg, -jnp.inf)).astype(BF16)
                coff = cm[:, ns] * eac[:, h:h + 1]
                lhs = jnp.concatenate([cb * lmat, coff], axis=1)
                part = _dot(lhs, rhs_half[half][:, pair])
                y = part if y is None else y + part
                wcols.append(jnp.broadcast_to(wdec[:, h:h + 1], (t, M2_HEADDIM)))
                scols.append(jnp.broadcast_to(sdec[:, h:h + 1], (1, M2_HEADDIM)))
            y_scr[:, pair] = y
        gs = slice(g * M2_HPG * M2_HEADDIM, (g + 1) * M2_HPG * M2_HEADDIM)
        xd = xs_bf[:, gs] * jnp.concatenate(wcols, axis=1)
        upd = lax.dot_general(bm[:, ns], xd, (((0,), (0,)), ((), ())),
                              preferred_element_type=F32)
        state_scr[:, gs] = state[:, gs] * jnp.concatenate(scols, axis=1) + upd

    yz = (y_scr[...] + dsk_ref[...] * xs_bf.astype(F32)) * zs_ref[0].astype(F32)
    o_ref[0] = _rms(yz, nw_ref[...]).astype(o_ref.dtype)


def _ssd(zs, xc, dt, dt_bias, a_log, d_skip, norm_w):
    b, l, _ = zs.shape
    t = M2_CHUNK
    cdim = xc.shape[2]
    dtt = jnp.swapaxes(dt, 1, 2)
    row = lambda a: a.reshape(1, -1)
    col = lambda a: a.reshape(-1, 1)
    seq = lambda w: pl.BlockSpec((1, t, w), lambda i, c: (i, c, 0))
    const = lambda a: pl.BlockSpec(a.shape, lambda i, c: (0, 0))
    args = [zs, xc, dt, dtt, row(dt_bias), col(dt_bias), row(a_log), col(a_log),
            row(jnp.repeat(d_skip, M2_HEADDIM)), row(norm_w)]
    in_specs = [seq(M2_DINNER), seq(cdim), seq(M2_HEADS),
                pl.BlockSpec((1, M2_HEADS, t), lambda i, c: (i, 0, c))]
    in_specs += [const(a) for a in args[4:]]
    return pl.pallas_call(
        _ssd_kernel,
        out_shape=jax.ShapeDtypeStruct((b, l, M2_DINNER), BF16),
        grid=(b, l // t),
        in_specs=in_specs,
        out_specs=seq(M2_DINNER),
        scratch_shapes=[pltpu.VMEM((M2_STATE, M2_DINNER), F32),
                        pltpu.VMEM((t, M2_DINNER), F32)],
        compiler_params=_params(("arbitrary", "arbitrary")),
        name="ssd",
    )(*args)


MOE_TILE = 512
META_I1, META_I2, META_R1, META_R2, META_G1, META_G2 = range(6)


def _router_kernel(h_ref, nw_ref, r_ref, hn_ref, meta_ref, cnt_ref, run_scr):
    @pl.when(pl.program_id(0) == 0)
    def _():
        run_scr[...] = jnp.zeros_like(run_scr)

    tm = h_ref.shape[0]
    lane = lax.broadcasted_iota(jnp.int32, (tm, LANES), 1).astype(F32)
    hn = _rms(h_ref[...], nw_ref[...])
    hn_ref[...] = hn
    logits = jnp.dot(hn, r_ref[...], preferred_element_type=F32,
                     precision=lax.Precision.HIGHEST)
    logits = jnp.where(lane < N_EXPERTS, logits, -jnp.inf)
    m1 = jnp.max(logits, axis=-1, keepdims=True)
    i1 = jnp.min(jnp.where(logits == m1, lane, float(LANES)), axis=-1, keepdims=True)
    rest = jnp.where(lane == i1, -jnp.inf, logits)
    m2 = jnp.max(rest, axis=-1, keepdims=True)
    i2 = jnp.min(jnp.where(rest == m2, lane, float(LANES)), axis=-1, keepdims=True)
    e2 = jnp.exp(m2 - m1)
    g1 = 1.0 / (1.0 + e2)
    g2 = e2 / (1.0 + e2)
    oh1 = (lane == i1).astype(F32)
    oh2 = (lane == i2).astype(F32)
    oh = oh1 + oh2
    ri = lax.broadcasted_iota(jnp.int32, (tm, tm), 0)
    ci = lax.broadcasted_iota(jnp.int32, (tm, tm), 1)
    before = _dot((ri > ci).astype(BF16), oh.astype(BF16)) + run_scr[0:1, :]
    r1 = jnp.sum(oh1 * before, axis=-1, keepdims=True)
    r2 = jnp.sum(oh2 * before, axis=-1, keepdims=True)
    run_scr[...] = run_scr[...] + jnp.sum(oh, axis=0, keepdims=True)
    cnt_ref[...] = run_scr[...]
    meta = jnp.zeros((tm, LANES), F32)
    for col, val in ((META_I1, i1), (META_I2, i2), (META_R1, r1), (META_R2, r2),
                     (META_G1, g1), (META_G2, g2)):
        meta = jnp.where(lane == float(col), val, meta)
    meta_ref[...] = meta


def _router(h, norm_w, router):
    n, d = h.shape
    tm = ROW_TILE
    router_p = jnp.zeros((d, LANES), F32).at[:, :router.shape[1]].set(router)
    return pl.pallas_call(
        _router_kernel,
        out_shape=(jax.ShapeDtypeStruct((n, d), F32),
                   jax.ShapeDtypeStruct((n, LANES), F32),
                   jax.ShapeDtypeStruct((SUBLANES, LANES), F32)),
        grid=(n // tm,),
        in_specs=[pl.BlockSpec((tm, d), lambda i: (i, 0)),
                  pl.BlockSpec((1, d), lambda i: (0, 0)),
                  pl.BlockSpec((d, LANES), lambda i: (0, 0))],
        out_specs=(pl.BlockSpec((tm, d), lambda i: (i, 0)),
                   pl.BlockSpec((tm, LANES), lambda i: (i, 0)),
                   pl.BlockSpec((SUBLANES, LANES), lambda i: (0, 0))),
        scratch_shapes=[pltpu.VMEM((SUBLANES, LANES), F32)],
        compiler_params=_params(("arbitrary",)),
        name="router",
    )(h, norm_w.reshape(1, d), router_p)


def _row_copy(src_hbm, src_row, dst_ref, dst_row, sem):
    return pltpu.make_async_copy(src_hbm.at[pl.ds(src_row, 1)], dst_ref.at[pl.ds(dst_row, 1)], sem)


def _gmm_kernel(te_ref, tv_ref, src_ref, nxt_ref, hn_hbm, w1_ref, w3_ref, w2_ref, o_ref,
                xg_scr, xb_scr, sems):
    del te_ref
    j = pl.program_id(0)
    f = pl.program_id(1)
    tg = xb_scr.shape[0]
    slot = lax.rem(j, 2)

    def gather(idx_ref, s):
        def issue(r, carry):
            _row_copy(hn_hbm, idx_ref[0, 0, r], xg_scr.at[s], r, sems.at[s]).start()
            return carry
        lax.fori_loop(0, tg, issue, 0, unroll=8)

    @pl.when(f == 0)
    def _():
        @pl.when(j == 0)
        def _():
            gather(src_ref, 0)

        def drain(r, carry):
            _row_copy(hn_hbm, 0, xg_scr.at[slot], 0, sems.at[slot]).wait()
            return carry
        lax.fori_loop(0, tg, drain, 0, unroll=8)

        @pl.when(j + 1 < pl.num_programs(0))
        def _():
            gather(nxt_ref, 1 - slot)

        xb_scr[...] = xg_scr[slot].astype(BF16)
        o_ref[...] = jnp.zeros_like(o_ref)

    @pl.when(tv_ref[j] != 0)
    def _():
        x = xb_scr[...]
        a = _dot(x, w1_ref[0])
        c = _dot(x, w3_ref[0])
        o_ref[...] += _dot((_silu(a) * c).astype(BF16), w2_ref[0])


def _grouped_swiglu(hn, src, tile_expert, tile_valid, w1, w3, w2, n_chunks):
    d = hn.shape[1]
    fc = w1.shape[2] // n_chunks
    tg = MOE_TILE
    n_tiles = src.shape[0] // tg
    src = src.reshape(n_tiles, 1, tg)
    grid_spec = pltpu.PrefetchScalarGridSpec(
        num_scalar_prefetch=2,
        grid=(n_tiles, n_chunks),
        in_specs=[pl.BlockSpec((1, 1, tg), lambda j, f, te, tv: (j, 0, 0),
                               memory_space=pltpu.SMEM),
                  pl.BlockSpec((1, 1, tg), lambda j, f, te, tv: (jnp.minimum(j + 1, n_tiles - 1), 0, 0),
                               memory_space=pltpu.SMEM),
                  pl.BlockSpec(memory_space=pl.ANY),
                  pl.BlockSpec((1, d, fc), lambda j, f, te, tv: (te[j], 0, f)),
                  pl.BlockSpec((1, d, fc), lambda j, f, te, tv: (te[j], 0, f)),
                  pl.BlockSpec((1, fc, d), lambda j, f, te, tv: (te[j], f, 0))],
        out_specs=pl.BlockSpec((tg, d), lambda j, f, te, tv: (j, 0)),
        scratch_shapes=[pltpu.VMEM((2, tg, d), F32), pltpu.VMEM((tg, d), BF16),
                        pltpu.SemaphoreType.DMA((2,))],
    )
    return pl.pallas_call(
        _gmm_kernel,
        out_shape=jax.ShapeDtypeStruct((n_tiles * tg, d), F32),
        grid_spec=grid_spec,
        compiler_params=_params(("arbitrary", "arbitrary")),
        name="moe_gmm",
    )(tile_expert, tile_valid, src, src, hn, w1, w3, w2)


def _combine_kernel(pos1_ref, pos2_ref, h_ref, meta_ref, fw_ref, ys_hbm, o_ref,
                    y1_scr, y2_scr, sems):
    rows = h_ref.shape[0]

    def issue(t, carry):
        _row_copy(ys_hbm, pos1_ref[0, 0, t], y1_scr, t, sems.at[0]).start()
        _row_copy(ys_hbm, pos2_ref[0, 0, t], y2_scr, t, sems.at[1]).start()
        return carry

    lax.fori_loop(0, rows, issue, 0, unroll=8)

    def drain(t, carry):
        _row_copy(ys_hbm, 0, y1_scr, 0, sems.at[0]).wait()
        _row_copy(ys_hbm, 0, y2_scr, 0, sems.at[1]).wait()
        return carry

    lax.fori_loop(0, rows, drain, 0, unroll=8)
    meta = meta_ref[...]
    g1 = meta[:, META_G1:META_G1 + 1]
    g2 = meta[:, META_G2:META_G2 + 1]
    o_ref[...] = _rms(h_ref[...] + g1 * y1_scr[...] + g2 * y2_scr[...], fw_ref[...])


def _combine(h, meta, pos1, pos2, ys, final_w):
    n, d = h.shape
    tm = ROW_TILE
    steps = n // tm
    idx = lambda: pl.BlockSpec((1, 1, tm), lambda i: (i, 0, 0), memory_space=pltpu.SMEM)
    return pl.pallas_call(
        _combine_kernel,
        out_shape=jax.ShapeDtypeStruct((n, d), F32),
        grid=(steps,),
        in_specs=[idx(), idx(),
                  pl.BlockSpec((tm, d), lambda i: (i, 0)),
                  pl.BlockSpec((tm, LANES), lambda i: (i, 0)),
                  pl.BlockSpec((1, d), lambda i: (0, 0)),
                  pl.BlockSpec(memory_space=pl.ANY)],
        out_specs=pl.BlockSpec((tm, d), lambda i: (i, 0)),
        scratch_shapes=[pltpu.VMEM((tm, d), F32), pltpu.VMEM((tm, d), F32),
                        pltpu.SemaphoreType.DMA((2,))],
        compiler_params=_params(("arbitrary",)),
        name="moe_combine",
    )(pos1.reshape(steps, 1, tm), pos2.reshape(steps, 1, tm), h, meta,
      final_w.reshape(1, d), ys)


def _moe(h, norm_w, router, w1, w3, w2, final_w, n_chunks):
    n, d = h.shape
    ne = w1.shape[0]
    tg = MOE_TILE
    hn, meta, counts = _router(h, norm_w, router)
    counts = counts[0, :ne].astype(jnp.int32)
    padded = ((counts + tg - 1) // tg) * tg
    ends = jnp.cumsum(padded)
    offs = ends - padded
    col = lambda c: meta[:, c].astype(jnp.int32)
    pos1 = offs[col(META_I1)] + col(META_R1)
    pos2 = offs[col(META_I2)] + col(META_R2)
    n_tiles = (2 * n) // tg + ne
    starts = jnp.arange(n_tiles, dtype=jnp.int32) * tg
    tile_expert = jnp.minimum(jnp.searchsorted(ends, starts, side='right'), ne - 1).astype(jnp.int32)
    tile_valid = (starts < ends[-1]).astype(jnp.int32)
    tok = jnp.arange(n, dtype=jnp.int32)
    src = jnp.zeros((n_tiles * tg,), jnp.int32).at[pos1].set(tok).at[pos2].set(tok)
    ys = _grouped_swiglu(hn, src, tile_expert, tile_valid, w1, w3, w2, n_chunks)
    return _combine(h, meta, pos1, pos2, ys, final_w)


def kernel(x, mem, positions, mem_norm, norm_mix, norm_xattn, norm_ffn, xa_wq, xa_wk, xa_wv, xa_wo, ev_w_in, ev_s5_lam_re, ev_s5_lam_im, ev_s5_log_dt, ev_s5_b_re, ev_s5_b_im, ev_s5_c_re, ev_s5_c_im, ev_s5_d, ev_s5_w_glu, ev_s5_b_glu, ev_w_out, ev_ffn_w1, ev_ffn_w3, ev_ffn_w2, od_w_in, od_conv_w, od_conv_b, od_dt_bias, od_a_log, od_d, od_norm, od_w_out, od_router, od_moe_w1, od_moe_w3, od_moe_w2, final_norm):
    b, l, d = x.shape
    n = b * l
    assert l % ROW_TILE == 0 and l % RET_CHUNK == 0 and l % M2_CHUNK == 0 and l % S5_STEPS == 0
    assert b == SUBLANES, "the S5 scan maps the batch onto the sublanes"
    bf = lambda a: a.astype(BF16)

    mem_k, mem_v = _memkv(mem, mem_norm, bf(xa_wk), bf(xa_wv))
    h = x.reshape(n, d)

    q, k, v, g, u = _in_proj(
        h, norm_mix[0], bf(ev_w_in[0]),
        (RET_QK, RET_QK, RET_V, RET_V, S5_WIDTH), (F32, F32, BF16, F32, F32))
    y_ret = _retention(q.reshape(b, l, -1), k.reshape(b, l, -1), v.reshape(b, l, -1),
                       g.reshape(b, l, -1), positions)
    u_tb = jnp.swapaxes(u.reshape(b, l, S5_WIDTH), 0, 1).reshape(n, S5_WIDTH)
    tables = _s5_tables(ev_s5_lam_re[0], ev_s5_lam_im[0], ev_s5_log_dt[0], ev_s5_b_re[0],
                        ev_s5_b_im[0], ev_s5_c_re[0], ev_s5_c_im[0])
    y_s5 = _s5(u_tb, b, tables, ev_s5_d[0], bf(ev_s5_w_glu[0]), ev_s5_b_glu[0])
    y_s5 = jnp.swapaxes(y_s5.reshape(l, b, S5_WIDTH), 0, 1).reshape(n, S5_WIDTH)
    w_out = bf(ev_w_out[0])
    h = _out_xattn(h, [y_ret.reshape(n, RET_V), y_s5], [w_out[:RET_V], w_out[RET_V:]],
                   norm_xattn[0], bf(xa_wq[0]), mem_k[0], mem_v[0], bf(xa_wo[0]), l)
    h = _ffn(h, norm_ffn[0], bf(ev_ffn_w1[0]), bf(ev_ffn_w3[0]), bf(ev_ffn_w2[0]), 2)

    zs, xc, dt = _m2_in_proj(h, norm_mix[1], bf(od_w_in[0]), od_conv_w[0], od_conv_b[0], l)
    y = _ssd(zs.reshape(b, l, -1), xc.reshape(b, l, -1), dt.reshape(b, l, -1),
             od_dt_bias[0], od_a_log[0], od_d[0], od_norm[0])
    h = _out_xattn(h, [y.reshape(n, M2_DINNER)], [bf(od_w_out[0])],
                   norm_xattn[1], bf(xa_wq[1]), mem_k[1], mem_v[1], bf(xa_wo[1]), l)
    out = _moe(h, norm_ffn[1], od_router[0], bf(od_moe_w1[0]), bf(od_moe_w3[0]),
               bf(od_moe_w2[0]), final_norm, 2)
    return out.reshape(b, l, d)
```

```python
import functools
import math

import jax
import jax.numpy as jnp
from jax import lax
from jax.experimental import pallas as pl
from jax.experimental.pallas import tpu as pltpu

F32 = jnp.float32
BF16 = jnp.bfloat16
EPS = 1e-6

LANES = 128
SUBLANES = 8
VMEM_LIMIT_BYTES = 56 * 1024 * 1024

RET_HEADS = 6
RET_DK = 64
RET_DV = 128
RET_QK = RET_HEADS * RET_DK
RET_V = RET_HEADS * RET_DV
RET_CHUNK = 128
RET_BATCH = 2
ROPE_THETA = 10000.0
S5_GROUP = 16
S5_GROUPS = 16
S5_STATE = 64
S5_WIDTH = S5_GROUP * S5_GROUPS
S5_STEPS = 64
M2_HEADDIM = 64
M2_HEADS = 32
M2_GROUPS = 4
M2_HPG = M2_HEADS // M2_GROUPS
M2_STATE = 128
M2_CONV = 4
M2_CHUNK = 128
M2_DINNER = M2_HEADS * M2_HEADDIM
M2_BC = M2_GROUPS * M2_STATE
XA_HEADS = 4
N_EXPERTS = 8
ROW_TILE = 512


def _params(semantics):
    return pltpu.CompilerParams(dimension_semantics=semantics,
                                vmem_limit_bytes=VMEM_LIMIT_BYTES)


def _rms(x, w):
    return x * lax.rsqrt(jnp.mean(x * x, axis=-1, keepdims=True) + EPS) * w


def _dot(a, b):
    return jnp.dot(a, b, preferred_element_type=F32)


def _split_bf16(x):
    hi = x.astype(BF16)
    lo = (x - hi.astype(F32)).astype(BF16)
    return hi, lo


def _silu(x):
    h = 0.5 * x
    return h + h * jnp.tanh(h)


def _memkv_kernel(mem_ref, nw_ref, wk_ref, wv_ref, k_ref, v_ref):
    m = _rms(mem_ref[0], nw_ref[...]).astype(BF16)
    k_ref[0, 0] = _dot(m, wk_ref[0]).astype(BF16)
    v_ref[0, 0] = _dot(m, wv_ref[0]).astype(BF16)


def _memkv(mem, mem_norm, wk, wv):
    b, nm, d = mem.shape
    depth = wk.shape[0]
    out = jax.ShapeDtypeStruct((depth, b, nm, d), BF16)
    return pl.pallas_call(
        _memkv_kernel,
        out_shape=(out, out),
        grid=(depth, b),
        in_specs=[
            pl.BlockSpec((1, nm, d), lambda l, i: (i, 0, 0)),
            pl.BlockSpec((1, d), lambda l, i: (0, 0)),
            pl.BlockSpec((1, d, d), lambda l, i: (l, 0, 0)),
            pl.BlockSpec((1, d, d), lambda l, i: (l, 0, 0)),
        ],
        out_specs=(pl.BlockSpec((1, 1, nm, d), lambda l, i: (l, i, 0, 0)),
                   pl.BlockSpec((1, 1, nm, d), lambda l, i: (l, i, 0, 0))),
        compiler_params=_params(("arbitrary", "arbitrary")),
        name="memkv",
    )(mem, mem_norm.reshape(1, d), wk, wv)


def _in_proj_kernel(bounds, x_ref, nw_ref, w_ref, *out_refs):
    hn = _rms(x_ref[...], nw_ref[...]).astype(BF16)
    for (lo, hi), o_ref in zip(bounds, out_refs):
        o_ref[...] = _dot(hn, w_ref[:, lo:hi]).astype(o_ref.dtype)


def _in_proj(x, norm_w, w, widths, dtypes):
    n, d = x.shape
    bounds, lo = [], 0
    for wd in widths:
        bounds.append((lo, lo + wd))
        lo += wd
    assert lo == w.shape[1]
    tm = ROW_TILE
    return pl.pallas_call(
        functools.partial(_in_proj_kernel, tuple(bounds)),
        out_shape=tuple(jax.ShapeDtypeStruct((n, wd), dt) for wd, dt in zip(widths, dtypes)),
        grid=(n // tm,),
        in_specs=[
            pl.BlockSpec((tm, d), lambda i: (i, 0)),
            pl.BlockSpec((1, d), lambda i: (0, 0)),
            pl.BlockSpec(w.shape, lambda i: (0, 0)),
        ],
        out_specs=tuple(pl.BlockSpec((tm, wd), lambda i: (i, 0)) for wd in widths),
        compiler_params=_params(("arbitrary",)),
        name="in_proj",
    )(x, norm_w.reshape(1, d), w)


def _ret_kernel(q_ref, k_ref, v_ref, g_ref, pos_ref, invf_ref, rot_ref, din_ref,
                dcr_ref, dst_ref, dch_ref, o_ref, state_ref):
    @pl.when(pl.program_id(1) == 0)
    def _():
        state_ref[...] = jnp.zeros_like(state_ref)

    reps = RET_QK // LANES
    dcr = dcr_ref[...]
    for bi in range(q_ref.shape[0]):
        ang = pos_ref[bi].astype(F32) * invf_ref[...]
        cos = jnp.concatenate([jnp.cos(ang)] * reps, axis=1)
        sin = jnp.concatenate([jnp.sin(ang)] * reps, axis=1)

        def rotary(x, cos=cos, sin=sin):
            hi, lo = _split_bf16(x)
            swapped = _dot(hi, rot_ref[...]) + _dot(lo, rot_ref[...])
            return x * cos + swapped * sin

        q = rotary(q_ref[bi]).astype(BF16)
        kf = rotary(k_ref[bi]) * (RET_DK ** -0.5)
        k = kf.astype(BF16)
        kd = (kf * dst_ref[...]).astype(BF16)
        v = v_ref[bi]
        g = g_ref[bi]
        state = state_ref[bi]
        state_bf = state.astype(BF16)
        for h in range(RET_HEADS):
            ks = slice(h * RET_DK, (h + 1) * RET_DK)
            vs = slice(h * RET_DV, (h + 1) * RET_DV)
            scores = lax.dot_general(q[:, ks], k[:, ks], (((1,), (1,)), ((), ())),
                                     preferred_element_type=F32) * din_ref[h]
            y = _dot(scores.astype(BF16), v[:, vs])
            y = y + _dot(q[:, ks], state_bf[:, vs]) * dcr[:, vs]
            upd = lax.dot_general(kd[:, ks], v[:, vs], (((0,), (0,)), ((), ())),
                                  preferred_element_type=F32)
            state_ref[bi, :, vs] = state[:, vs] * dch_ref[:, vs] + upd
            y = y * lax.rsqrt(jnp.mean(y * y, axis=-1, keepdims=True) + EPS)
            o_ref[bi, :, vs] = (y * _silu(g[:, vs])).astype(o_ref.dtype)


def _retention_tables():
    t = RET_CHUNK
    log_gamma = jnp.log1p(-(2.0 ** (-5.0 - jnp.arange(RET_HEADS, dtype=F32))))
    idx = jnp.arange(t, dtype=F32)
    diff = idx[:, None] - idx[None, :]
    d_inner = jnp.where(diff[None] >= 0,
                        jnp.exp(jnp.maximum(diff, 0.0)[None] * log_gamma[:, None, None]), 0.0)
    d_cross = jnp.exp((idx[:, None] + 1.0) * log_gamma)
    d_state = jnp.exp((t - 1.0 - idx)[:, None] * log_gamma)
    d_chunk = jnp.exp(t * log_gamma)
    d_cross = jnp.repeat(d_cross, RET_DV, axis=1)
    d_state = jnp.repeat(d_state, RET_DK, axis=1)
    d_chunk = jnp.repeat(d_chunk, RET_DV)[None, :]
    half = RET_DK // 2
    inv_freq = ROPE_THETA ** (-jnp.arange(half, dtype=F32) / half)
    inv_freq = jnp.tile(inv_freq, LANES // half)[None, :]
    col = jnp.arange(RET_QK)
    first = (col % RET_DK) < half
    src = jnp.where(first, col + half, col - half)
    rot = jnp.zeros((RET_QK, RET_QK), F32).at[src, col].set(jnp.where(first, -1.0, 1.0))
    return inv_freq, rot.astype(BF16), d_inner, d_cross, d_state, d_chunk


def _retention(q, k, v, g, positions):
    b, l, _ = q.shape
    t = RET_CHUNK
    inv_freq, rot, d_inner, d_cross, d_state, d_chunk = _retention_tables()
    nb = RET_BATCH
    seq = lambda w: pl.BlockSpec((nb, t, w), lambda i, c: (i, c, 0))
    const2 = lambda a: pl.BlockSpec(a.shape, lambda i, c: (0, 0))
    return pl.pallas_call(
        _ret_kernel,
        out_shape=jax.ShapeDtypeStruct((b, l, RET_V), BF16),
        grid=(b // nb, l // t),
        in_specs=[seq(RET_QK), seq(RET_QK), seq(RET_V), seq(RET_V), seq(1),
                  const2(inv_freq), const2(rot),
                  pl.BlockSpec(d_inner.shape, lambda i, c: (0, 0, 0)),
                  const2(d_cross), const2(d_state), const2(d_chunk)],
        out_specs=seq(RET_V),
        scratch_shapes=[pltpu.VMEM((nb, RET_DK, RET_V), F32)],
        compiler_params=_params(("arbitrary", "arbitrary")),
        name="retention",
    )(q, k, v, g, positions.reshape(b, l, 1), inv_freq, rot, d_inner, d_cross, d_state, d_chunk)


def _s5_kernel(u_ref, bin_ref, are_ref, aim_ref, cout_ref, d_ref, wg_ref, bg_ref,
               o_ref, x_scr, st_ref):
    @pl.when(pl.program_id(0) == 0)
    def _():
        st_ref[...] = jnp.zeros_like(st_ref)

    nstate = S5_GROUPS * S5_STATE
    u = u_ref[...]
    x_scr[...] = _dot(u.astype(BF16), bin_ref[...])
    rows = st_ref.shape[1]
    are = jnp.broadcast_to(are_ref[...], (rows, nstate))
    aim = jnp.broadcast_to(aim_ref[...], (rows, nstate))

    def step(t, carry):
        sr, si = carry
        r0 = pl.multiple_of(t * rows, rows)
        xr = x_scr[pl.ds(r0, rows), :nstate]
        xi = x_scr[pl.ds(r0, rows), nstate:]
        nr = are * sr - aim * si + xr
        ni = are * si + aim * sr + xi
        x_scr[pl.ds(r0, rows), :nstate] = nr
        x_scr[pl.ds(r0, rows), nstate:] = ni
        return nr, ni

    sr, si = lax.fori_loop(0, x_scr.shape[0] // rows, step, (st_ref[0], st_ref[1]))
    st_ref[0] = sr
    st_ref[1] = si
    hi, lo = _split_bf16(x_scr[...])
    y = _dot(hi, cout_ref[...]) + _dot(lo, cout_ref[...]) + d_ref[...] * u
    y = jax.nn.gelu(y)
    gate = _dot(y.astype(BF16), wg_ref[...]) + bg_ref[...]
    o_ref[...] = (y * (1.0 / (1.0 + jnp.exp(-gate)))).astype(o_ref.dtype)


def _s5_tables(lam_re, lam_im, log_dt, b_re, b_im, c_re, c_im):
    dt = jnp.exp(log_dt)[:, None]
    mag = jnp.exp(lam_re * dt)
    ab_re, ab_im = mag * jnp.cos(lam_im * dt), mag * jnp.sin(lam_im * dt)
    den = lam_re * lam_re + lam_im * lam_im
    nr, ni = ab_re - 1.0, ab_im
    f_re = (nr * lam_re + ni * lam_im) / den
    f_im = (ni * lam_re - nr * lam_im) / den
    bb_re = f_re[..., None] * b_re - f_im[..., None] * b_im
    bb_im = f_re[..., None] * b_im + f_im[..., None] * b_re
    eye = jnp.eye(S5_GROUPS, dtype=F32)
    bd = lambda m: jnp.einsum('gpc,gh->gchp', m, eye).reshape(S5_WIDTH, S5_GROUPS * S5_STATE)
    b_in = jnp.concatenate([bd(bb_re), bd(bb_im)], axis=1)
    cd = lambda m: jnp.einsum('gcp,gh->gphc', m, eye).reshape(S5_GROUPS * S5_STATE, S5_WIDTH)
    c_out = jnp.concatenate([cd(c_re), -cd(c_im)], axis=0)
    return (b_in.astype(BF16), ab_re.reshape(1, -1), ab_im.reshape(1, -1), c_out.astype(BF16))


def _s5(u_tb, batch, tables, d_skip, w_glu, b_glu):
    rows_total, width = u_tb.shape
    b_in, a_re, a_im, c_out = tables
    nstate = S5_GROUPS * S5_STATE
    tr = S5_STEPS * batch
    const = lambda a: pl.BlockSpec(a.shape, lambda i: (0, 0))
    d_skip = d_skip.reshape(1, width)
    b_glu = b_glu.reshape(1, width)
    return pl.pallas_call(
        _s5_kernel,
        out_shape=jax.ShapeDtypeStruct((rows_total, width), BF16),
        grid=(rows_total // tr,),
        in_specs=[pl.BlockSpec((tr, width), lambda i: (i, 0)),
                  const(b_in), const(a_re), const(a_im), const(c_out),
                  const(d_skip), const(w_glu), const(b_glu)],
        out_specs=pl.BlockSpec((tr, width), lambda i: (i, 0)),
        scratch_shapes=[pltpu.VMEM((tr, 2 * nstate), F32),
                        pltpu.VMEM((2, batch, nstate), F32)],
        compiler_params=_params(("arbitrary",)),
        name="s5",
    )(u_tb, b_in, a_re, a_im, c_out, d_skip, w_glu, b_glu)


def _out_xattn_kernel(n_y, *refs):
    res_ref = refs[0]
    y_refs = refs[1:1 + n_y]
    w_refs = refs[1 + n_y:1 + 2 * n_y]
    nx_ref, wq_ref, k_ref, v_ref, wo_ref, o_ref = refs[1 + 2 * n_y:]
    h = res_ref[...]
    for y_ref, w_ref in zip(y_refs, w_refs):
        h = h + _dot(y_ref[...], w_ref[...])
    d = h.shape[1]
    hd = d // XA_HEADS
    q = (_dot(_rms(h, nx_ref[...]).astype(BF16), wq_ref[...]) * (hd ** -0.5)).astype(BF16)
    k = k_ref[0]
    v = v_ref[0]
    outs = []
    for i in range(XA_HEADS):
        hs = slice(i * hd, (i + 1) * hd)
        s = lax.dot_general(q[:, hs], k[:, hs], (((1,), (1,)), ((), ())),
                            preferred_element_type=F32)
        p = jnp.exp(s - jnp.max(s, axis=-1, keepdims=True))
        p = p / jnp.sum(p, axis=-1, keepdims=True)
        outs.append(_dot(p.astype(BF16), v[:, hs]).astype(BF16))
    o = jnp.concatenate(outs, axis=1)
    o_ref[...] = h + _dot(o, wo_ref[...])


def _out_xattn(res, ys, ws, norm_w, wq, k, v, wo, seq_len):
    n, d = res.shape
    tm = ROW_TILE
    nm = k.shape[1]
    tiles_per_seq = seq_len // tm
    row = lambda w: pl.BlockSpec((tm, w), lambda i: (i, 0))
    const = lambda a: pl.BlockSpec(a.shape, lambda i: (0, 0))
    kv = pl.BlockSpec((1, nm, d), lambda i: (i // tiles_per_seq, 0, 0))
    return pl.pallas_call(
        functools.partial(_out_xattn_kernel, len(ys)),
        out_shape=jax.ShapeDtypeStruct((n, d), F32),
        grid=(n // tm,),
        in_specs=[row(d)] + [row(y.shape[1]) for y in ys] + [const(w) for w in ws]
                 + [pl.BlockSpec((1, d), lambda i: (0, 0)), const(wq), kv, kv, const(wo)],
        out_specs=row(d),
        compiler_params=_params(("arbitrary",)),
        name="out_xattn",
    )(res, *ys, *ws, norm_w.reshape(1, d), wq, k, v, wo)


def _ffn_kernel(h_ref, nw_ref, w1_ref, w3_ref, w2_ref, o_ref, hn_scr):
    f = pl.program_id(1)

    @pl.when(f == 0)
    def _():
        h = h_ref[...]
        hn_scr[...] = _rms(h, nw_ref[...]).astype(BF16)
        o_ref[...] = h

    hn = hn_scr[...]
    a = _dot(hn, w1_ref[...])
    c = _dot(hn, w3_ref[...])
    o_ref[...] += _dot((_silu(a) * c).astype(BF16), w2_ref[...])


def _ffn(h, norm_w, w1, w3, w2, n_chunks):
    n, d = h.shape
    f = w1.shape[1]
    fc = f // n_chunks
    tm = ROW_TILE
    return pl.pallas_call(
        _ffn_kernel,
        out_shape=jax.ShapeDtypeStruct((n, d), F32),
        grid=(n // tm, n_chunks),
        in_specs=[pl.BlockSpec((tm, d), lambda i, j: (i, 0)),
                  pl.BlockSpec((1, d), lambda i, j: (0, 0)),
                  pl.BlockSpec((d, fc), lambda i, j: (0, j)),
                  pl.BlockSpec((d, fc), lambda i, j: (0, j)),
                  pl.BlockSpec((fc, d), lambda i, j: (j, 0))],
        out_specs=pl.BlockSpec((tm, d), lambda i, j: (i, 0)),
        scratch_shapes=[pltpu.VMEM((tm, d), BF16)],
        compiler_params=_params(("arbitrary", "arbitrary")),
        name="ffn",
    )(h, norm_w.reshape(1, d), w1, w3, w2)


def _softplus(x):
    return jnp.maximum(x, 0.0) + jnp.log1p(jnp.exp(-jnp.abs(x)))


def _m2_in_kernel(tiles_per_seq, x_ref, nw_ref, w_ref, cw_ref, cb_ref, zs_ref, xc_ref, dt_ref,
                  tail_scr):
    tm = x_ref.shape[0]
    pad = SUBLANES
    cdim = xc_ref.shape[1]
    hn = _rms(x_ref[...], nw_ref[...]).astype(BF16)

    @pl.when(pl.program_id(0) % tiles_per_seq == 0)
    def _():
        tail_scr[...] = jnp.zeros_like(tail_scr)

    blk = 4 * LANES
    for c0 in range(0, cdim, blk):
        cs = slice(c0, c0 + blk)
        xbc = _dot(hn, w_ref[:, M2_DINNER + c0:M2_DINNER + c0 + blk])
        if c0 < M2_DINNER:
            zs_ref[:, cs] = _silu(_dot(hn, w_ref[:, cs])).astype(zs_ref.dtype)
        ext = jnp.concatenate([tail_scr[:, cs], xbc], axis=0)
        acc = cb_ref[:, cs] + cw_ref[M2_CONV - 1:M2_CONV, cs] * xbc
        for j in range(M2_CONV - 1):
            shifted = pltpu.roll(ext, M2_CONV - 1 - j, 0)[pad:pad + tm, :]
            acc = acc + cw_ref[j:j + 1, cs] * shifted
        xc_ref[:, cs] = _silu(acc).astype(xc_ref.dtype)
        tail_scr[:, cs] = xbc[tm - pad:, :]
    dt_ref[...] = _dot(hn, w_ref[:, M2_DINNER + cdim:])


def _m2_in_proj(x, norm_w, w, conv_w, conv_b, seq_len):
    n, d = x.shape
    cdim = conv_w.shape[1]
    tm = ROW_TILE
    row = lambda wd: pl.BlockSpec((tm, wd), lambda i: (i, 0))
    const = lambda a: pl.BlockSpec(a.shape, lambda i: (0, 0))
    conv_b = conv_b.reshape(1, cdim)
    return pl.pallas_call(
        functools.partial(_m2_in_kernel, seq_len // tm),
        out_shape=(jax.ShapeDtypeStruct((n, M2_DINNER), BF16),
                   jax.ShapeDtypeStruct((n, cdim), BF16),
                   jax.ShapeDtypeStruct((n, M2_HEADS), F32)),
        grid=(n // tm,),
        in_specs=[row(d), pl.BlockSpec((1, d), lambda i: (0, 0)), const(w), const(conv_w),
                  const(conv_b)],
        out_specs=(row(M2_DINNER), row(cdim), row(M2_HEADS)),
        scratch_shapes=[pltpu.VMEM((SUBLANES, cdim), F32)],
        compiler_params=_params(("arbitrary",)),
        name="m2_in_proj",
    )(x, norm_w.reshape(1, d), w, conv_w, conv_b)


def _ssd_kernel(zs_ref, xc_ref, dt_ref, dtt_ref, dtb_ref, dtbt_ref, alog_ref, alogt_ref,
                dsk_ref, nw_ref, o_ref, state_scr, y_scr):
    t = M2_CHUNK

    @pl.when(pl.program_id(1) == 0)
    def _():
        state_scr[...] = jnp.zeros_like(state_scr)

    xc = xc_ref[0]
    xs_bf = xc[:, :M2_DINNER]
    bm = xc[:, M2_DINNER:M2_DINNER + M2_BC]
    cm = xc[:, M2_DINNER + M2_BC:]

    dt = _softplus(dt_ref[0] + dtb_ref[...])
    adt = dt * (-jnp.exp(alog_ref[...]))
    dtt = _softplus(dtt_ref[0] + dtbt_ref[...])
    adtt = dtt * (-jnp.exp(alogt_ref[...]))
    ri = lax.broadcasted_iota(jnp.int32, (t, t), 0)
    ci = lax.broadcasted_iota(jnp.int32, (t, t), 1)
    causal = ri >= ci
    tri = causal.astype(F32)
    acum = jnp.dot(tri, adt, preferred_element_type=F32, precision=lax.Precision.HIGHEST)
    acumt = lax.dot_general(adtt, tri, (((1,), (1,)), ((), ())), preferred_element_type=F32,
                            precision=lax.Precision.HIGHEST)
    arow = acumt - jnp.log(dtt)
    alast = acum[t - 1:t, :]
    wdec = (dt * jnp.exp(alast - acum)).astype(BF16)
    eac = jnp.exp(acum).astype(BF16)
    sdec = jnp.exp(alast)

    state = state_scr[...]
    state_bf = state.astype(BF16)
    lane = lax.broadcasted_iota(jnp.int32, (t, M2_DINNER), 1)
    even = (lane % (2 * M2_HEADDIM)) < M2_HEADDIM
    zero = jnp.zeros((), BF16)
    rhs_half = (jnp.concatenate([jnp.where(even, xs_bf, zero), jnp.where(even, state_bf, zero)], axis=0),
                jnp.concatenate([jnp.where(even, zero, xs_bf), jnp.where(even, zero, state_bf)], axis=0))
    for g in range(M2_GROUPS):
        ns = slice(g * M2_STATE, (g + 1) * M2_STATE)
        cb = lax.dot_general(cm[:, ns], bm[:, ns], (((1,), (1,)), ((), ())),
                             preferred_element_type=F32).astype(BF16)
        wcols, scols = [], []
        for kp in range(M2_HPG // 2):
            pair = slice((g * M2_HPG + 2 * kp) * M2_HEADDIM, (g * M2_HPG + 2 * kp + 2) * M2_HEADDIM)
            y = None
            for half in range(2):
                h = g * M2_HPG + 2 * kp + half
                seg = acum[:, h:h + 1] - arow[h:h + 1, :]
                lmat = jnp.exp(jnp.where(causal, seg, -jnp.inf)).astype(BF16)
                coff = cm[:, ns] * eac[:, h:h + 1]
                lhs = jnp.concatenate([cb * lmat, coff], axis=1)
                part = _dot(lhs, rhs_half[half][:, pair])
                y = part if y is None else y + part
                wcols.append(jnp.broadcast_to(wdec[:, h:h + 1], (t, M2_HEADDIM)))
                scols.append(jnp.broadcast_to(sdec[:, h:h + 1], (1, M2_HEADDIM)))
            y_scr[:, pair] = y
        gs = slice(g * M2_HPG * M2_HEADDIM, (g + 1) * M2_HPG * M2_HEADDIM)
        xd = xs_bf[:, gs] * jnp.concatenate(wcols, axis=1)
        upd = lax.dot_general(bm[:, ns], xd, (((0,), (0,)), ((), ())),
                              preferred_element_type=F32)
        state_scr[:, gs] = state[:, gs] * jnp.concatenate(scols, axis=1) + upd

    yz = (y_scr[...] + dsk_ref[...] * xs_bf.astype(F32)) * zs_ref[0].astype(F32)
    o_ref[0] = _rms(yz, nw_ref[...]).astype(o_ref.dtype)


def _ssd(zs, xc, dt, dt_bias, a_log, d_skip, norm_w):
    b, l, _ = zs.shape
    t = M2_CHUNK
    cdim = xc.shape[2]
    dtt = jnp.swapaxes(dt, 1, 2)
    row = lambda a: a.reshape(1, -1)
    col = lambda a: a.reshape(-1, 1)
    seq = lambda w: pl.BlockSpec((1, t, w), lambda i, c: (i, c, 0))
    const = lambda a: pl.BlockSpec(a.shape, lambda i, c: (0, 0))
    args = [zs, xc, dt, dtt, row(dt_bias), col(dt_bias), row(a_log), col(a_log),
            row(jnp.repeat(d_skip, M2_HEADDIM)), row(norm_w)]
    in_specs = [seq(M2_DINNER), seq(cdim), seq(M2_HEADS),
                pl.BlockSpec((1, M2_HEADS, t), lambda i, c: (i, 0, c))]
    in_specs += [const(a) for a in args[4:]]
    return pl.pallas_call(
        _ssd_kernel,
        out_shape=jax.ShapeDtypeStruct((b, l, M2_DINNER), BF16),
        grid=(b, l // t),
        in_specs=in_specs,
        out_specs=seq(M2_DINNER),
        scratch_shapes=[pltpu.VMEM((M2_STATE, M2_DINNER), F32),
                        pltpu.VMEM((t, M2_DINNER), F32)],
        compiler_params=_params(("arbitrary", "arbitrary")),
        name="ssd",
    )(*args)


MOE_TILE = 512
META_I1, META_I2, META_R1, META_R2, META_G1, META_G2 = range(6)


def _router_kernel(h_ref, nw_ref, r_ref, hn_ref, meta_ref, cnt_ref, run_scr):
    @pl.when(pl.program_id(0) == 0)
    def _():
        run_scr[...] = jnp.zeros_like(run_scr)

    tm = h_ref.shape[0]
    lane = lax.broadcasted_iota(jnp.int32, (tm, LANES), 1).astype(F32)
    hn = _rms(h_ref[...], nw_ref[...])
    hn_ref[...] = hn
    logits = jnp.dot(hn, r_ref[...], preferred_element_type=F32,
                     precision=lax.Precision.HIGHEST)
    logits = jnp.where(lane < N_EXPERTS, logits, -jnp.inf)
    m1 = jnp.max(logits, axis=-1, keepdims=True)
    i1 = jnp.min(jnp.where(logits == m1, lane, float(LANES)), axis=-1, keepdims=True)
    rest = jnp.where(lane == i1, -jnp.inf, logits)
    m2 = jnp.max(rest, axis=-1, keepdims=True)
    i2 = jnp.min(jnp.where(rest == m2, lane, float(LANES)), axis=-1, keepdims=True)
    e2 = jnp.exp(m2 - m1)
    g1 = 1.0 / (1.0 + e2)
    g2 = e2 / (1.0 + e2)
    oh1 = (lane == i1).astype(F32)
    oh2 = (lane == i2).astype(F32)
    oh = oh1 + oh2
    ri = lax.broadcasted_iota(jnp.int32, (tm, tm), 0)
    ci = lax.broadcasted_iota(jnp.int32, (tm, tm), 1)
    before = _dot((ri > ci).astype(BF16), oh.astype(BF16)) + run_scr[0:1, :]
    r1 = jnp.sum(oh1 * before, axis=-1, keepdims=True)
    r2 = jnp.sum(oh2 * before, axis=-1, keepdims=True)
    run_scr[...] = run_scr[...] + jnp.sum(oh, axis=0, keepdims=True)
    cnt_ref[...] = run_scr[...]
    meta = jnp.zeros((tm, LANES), F32)
    for col, val in ((META_I1, i1), (META_I2, i2), (META_R1, r1), (META_R2, r2),
                     (META_G1, g1), (META_G2, g2)):
        meta = jnp.where(lane == float(col), val, meta)
    meta_ref[...] = meta


def _router(h, norm_w, router):
    n, d = h.shape
    tm = ROW_TILE
    router_p = jnp.zeros((d, LANES), F32).at[:, :router.shape[1]].set(router)
    return pl.pallas_call(
        _router_kernel,
        out_shape=(jax.ShapeDtypeStruct((n, d), F32),
                   jax.ShapeDtypeStruct((n, LANES), F32),
                   jax.ShapeDtypeStruct((SUBLANES, LANES), F32)),
        grid=(n // tm,),
        in_specs=[pl.BlockSpec((tm, d), lambda i: (i, 0)),
                  pl.BlockSpec((1, d), lambda i: (0, 0)),
                  pl.BlockSpec((d, LANES), lambda i: (0, 0))],
        out_specs=(pl.BlockSpec((tm, d), lambda i: (i, 0)),
                   pl.BlockSpec((tm, LANES), lambda i: (i, 0)),
                   pl.BlockSpec((SUBLANES, LANES), lambda i: (0, 0))),
        scratch_shapes=[pltpu.VMEM((SUBLANES, LANES), F32)],
        compiler_params=_params(("arbitrary",)),
        name="router",
    )(h, norm_w.reshape(1, d), router_p)


def _row_copy(src_hbm, src_row, dst_ref, dst_row, sem):
    return pltpu.make_async_copy(src_hbm.at[pl.ds(src_row, 1)], dst_ref.at[pl.ds(dst_row, 1)], sem)


def _gmm_kernel(n_chunks, te_ref, tv_ref, src_ref, nxt_ref, hn_hbm, w1_ref, w3_ref, w2_ref, o_ref,
                xg_scr, sems):
    del te_ref
    j = pl.program_id(0)
    tg = xg_scr.shape[1]
    fc = w1_ref.shape[2] // n_chunks
    slot = lax.rem(j, 2)

    @pl.when(j == 0)
    def _():
        def issue(r, carry):
            _row_copy(hn_hbm, src_ref[0, 0, r], xg_scr.at[0], r, sems.at[0]).start()
            return carry
        lax.fori_loop(0, tg, issue, 0, unroll=8)

    @pl.when(jnp.logical_or(j == 0, tv_ref[jnp.maximum(j - 1, 0)] != 0))
    def _():
        def drain(r, carry):
            _row_copy(hn_hbm, 0, xg_scr.at[slot], 0, sems.at[slot]).wait()
            return carry
        lax.fori_loop(0, tg, drain, 0, unroll=8)

    @pl.when(tv_ref[j] != 0)
    def _():
        for r in range(tg):
            _row_copy(hn_hbm, nxt_ref[0, 0, r], xg_scr.at[1 - slot], r, sems.at[1 - slot]).start()
        x = xg_scr[slot].astype(BF16)
        for c in range(n_chunks):
            fs = slice(c * fc, (c + 1) * fc)
            a = _dot(x, w1_ref[0, :, fs])
            b = _dot(x, w3_ref[0, :, fs])
            part = _dot((_silu(a) * b).astype(BF16), w2_ref[0, fs, :])
            if c == 0:
                o_ref[...] = part
            else:
                o_ref[...] += part

    @pl.when(tv_ref[j] == 0)
    def _():
        o_ref[...] = jnp.zeros_like(o_ref)


def _grouped_swiglu(hn, src, tile_expert, tile_valid, w1, w3, w2, n_chunks):
    d = hn.shape[1]
    f = w1.shape[2]
    tg = MOE_TILE
    n_tiles = src.shape[0] // tg
    src = src.reshape(n_tiles, 1, tg)
    expert = lambda shape: pl.BlockSpec(shape, lambda j, te, tv: (te[j], 0, 0),
                                        pipeline_mode=pl.Buffered(1))
    grid_spec = pltpu.PrefetchScalarGridSpec(
        num_scalar_prefetch=2,
        grid=(n_tiles,),
        in_specs=[pl.BlockSpec((1, 1, tg), lambda j, te, tv: (j, 0, 0), memory_space=pltpu.SMEM),
                  pl.BlockSpec((1, 1, tg), lambda j, te, tv: (jnp.minimum(j + 1, n_tiles - 1), 0, 0),
                               memory_space=pltpu.SMEM),
                  pl.BlockSpec(memory_space=pl.ANY),
                  expert((1, d, f)), expert((1, d, f)), expert((1, f, d))],
        out_specs=pl.BlockSpec((tg, d), lambda j, te, tv: (j, 0)),
        scratch_shapes=[pltpu.VMEM((2, tg, d), F32), pltpu.SemaphoreType.DMA((2,))],
    )
    return pl.pallas_call(
        functools.partial(_gmm_kernel, n_chunks),
        out_shape=jax.ShapeDtypeStruct((n_tiles * tg, d), F32),
        grid_spec=grid_spec,
        compiler_params=_params(("arbitrary",)),
        name="moe_gmm",
    )(tile_expert, tile_valid, src, src, hn, w1, w3, w2)


def _combine_kernel(pos1_ref, pos2_ref, h_ref, meta_ref, fw_ref, ys_hbm, o_ref,
                    y1_scr, y2_scr, sems):
    rows = h_ref.shape[0]
    for t in range(rows):
        _row_copy(ys_hbm, pos1_ref[0, 0, t], y1_scr, t, sems.at[0]).start()
        _row_copy(ys_hbm, pos2_ref[0, 0, t], y2_scr, t, sems.at[1]).start()
    for t in range(rows):
        _row_copy(ys_hbm, 0, y1_scr, t, sems.at[0]).wait()
        _row_copy(ys_hbm, 0, y2_scr, t, sems.at[1]).wait()
    meta = meta_ref[...]
    g1 = meta[:, META_G1:META_G1 + 1]
    g2 = meta[:, META_G2:META_G2 + 1]
    o_ref[...] = _rms(h_ref[...] + g1 * y1_scr[...] + g2 * y2_scr[...], fw_ref[...])


def _combine(h, meta, pos1, pos2, ys, final_w):
    n, d = h.shape
    tm = ROW_TILE
    steps = n // tm
    idx = lambda: pl.BlockSpec((1, 1, tm), lambda i: (i, 0, 0), memory_space=pltpu.SMEM)
    return pl.pallas_call(
        _combine_kernel,
        out_shape=jax.ShapeDtypeStruct((n, d), F32),
        grid=(steps,),
        in_specs=[idx(), idx(),
                  pl.BlockSpec((tm, d), lambda i: (i, 0)),
                  pl.BlockSpec((tm, LANES), lambda i: (i, 0)),
                  pl.BlockSpec((1, d), lambda i: (0, 0)),
                  pl.BlockSpec(memory_space=pl.ANY)],
        out_specs=pl.BlockSpec((tm, d), lambda i: (i, 0)),
        scratch_shapes=[pltpu.VMEM((tm, d), F32), pltpu.VMEM((tm, d), F32),
                        pltpu.SemaphoreType.DMA((2,))],
        compiler_params=_params(("arbitrary",)),
        name="moe_combine",
    )(pos1.reshape(steps, 1, tm), pos2.reshape(steps, 1, tm), h, meta,
      final_w.reshape(1, d), ys)


def _moe(h, norm_w, router, w1, w3, w2, final_w, n_chunks):
    n, d = h.shape
    ne = w1.shape[0]
    tg = MOE_TILE
    hn, meta, counts = _router(h, norm_w, router)
    counts = counts[0, :ne].astype(jnp.int32)
    padded = ((counts + tg - 1) // tg) * tg
    ends = jnp.cumsum(padded)
    offs = ends - padded
    col = lambda c: meta[:, c].astype(jnp.int32)
    pos1 = offs[col(META_I1)] + col(META_R1)
    pos2 = offs[col(META_I2)] + col(META_R2)
    n_tiles = (2 * n) // tg + ne + 1
    starts = jnp.arange(n_tiles, dtype=jnp.int32) * tg
    tile_expert = jnp.minimum(jnp.searchsorted(ends, starts, side='right'), ne - 1).astype(jnp.int32)
    tile_valid = (starts < ends[-1]).astype(jnp.int32)
    tok = jnp.arange(n, dtype=jnp.int32)
    src = jnp.zeros((n_tiles * tg,), jnp.int32).at[jnp.concatenate([pos1, pos2])].set(
        jnp.concatenate([tok, tok]), unique_indices=True)
    ys = _grouped_swiglu(hn, src, tile_expert, tile_valid, w1, w3, w2, n_chunks)
    return _combine(h, meta, pos1, pos2, ys, final_w)


def kernel(x, mem, positions, mem_norm, norm_mix, norm_xattn, norm_ffn, xa_wq, xa_wk, xa_wv, xa_wo, ev_w_in, ev_s5_lam_re, ev_s5_lam_im, ev_s5_log_dt, ev_s5_b_re, ev_s5_b_im, ev_s5_c_re, ev_s5_c_im, ev_s5_d, ev_s5_w_glu, ev_s5_b_glu, ev_w_out, ev_ffn_w1, ev_ffn_w3, ev_ffn_w2, od_w_in, od_conv_w, od_conv_b, od_dt_bias, od_a_log, od_d, od_norm, od_w_out, od_router, od_moe_w1, od_moe_w3, od_moe_w2, final_norm):
    b, l, d = x.shape
    n = b * l
    assert l % ROW_TILE == 0 and l % RET_CHUNK == 0 and l % M2_CHUNK == 0 and l % S5_STEPS == 0
    assert b == SUBLANES, "the S5 scan maps the batch onto the sublanes"
    bf = lambda a: a.astype(BF16)

    mem_k, mem_v = _memkv(mem, mem_norm, bf(xa_wk), bf(xa_wv))
    h = x.reshape(n, d)

    q, k, v, g, u = _in_proj(
        h, norm_mix[0], bf(ev_w_in[0]),
        (RET_QK, RET_QK, RET_V, RET_V, S5_WIDTH), (F32, F32, BF16, F32, F32))
    y_ret = _retention(q.reshape(b, l, -1), k.reshape(b, l, -1), v.reshape(b, l, -1),
                       g.reshape(b, l, -1), positions)
    u_tb = jnp.swapaxes(u.reshape(b, l, S5_WIDTH), 0, 1).reshape(n, S5_WIDTH)
    tables = _s5_tables(ev_s5_lam_re[0], ev_s5_lam_im[0], ev_s5_log_dt[0], ev_s5_b_re[0],
                        ev_s5_b_im[0], ev_s5_c_re[0], ev_s5_c_im[0])
    y_s5 = _s5(u_tb, b, tables, ev_s5_d[0], bf(ev_s5_w_glu[0]), ev_s5_b_glu[0])
    y_s5 = jnp.swapaxes(y_s5.reshape(l, b, S5_WIDTH), 0, 1).reshape(n, S5_WIDTH)
    w_out = bf(ev_w_out[0])
    h = _out_xattn(h, [y_ret.reshape(n, RET_V), y_s5], [w_out[:RET_V], w_out[RET_V:]],
                   norm_xattn[0], bf(xa_wq[0]), mem_k[0], mem_v[0], bf(xa_wo[0]), l)
    h = _ffn(h, norm_ffn[0], bf(ev_ffn_w1[0]), bf(ev_ffn_w3[0]), bf(ev_ffn_w2[0]), 2)

    zs, xc, dt = _m2_in_proj(h, norm_mix[1], bf(od_w_in[0]), od_conv_w[0], od_conv_b[0], l)
    y = _ssd(zs.reshape(b, l, -1), xc.reshape(b, l, -1), dt.reshape(b, l, -1),
             od_dt_bias[0], od_a_log[0], od_d[0], od_norm[0])
    h = _out_xattn(h, [y.reshape(n, M2_DINNER)], [bf(od_w_out[0])],
                   norm_xattn[1], bf(xa_wq[1]), mem_k[1], mem_v[1], bf(xa_wo[1]), l)
    out = _moe(h, norm_ffn[1], od_router[0], bf(od_moe_w1[0]), bf(od_moe_w3[0]),
               bf(od_moe_w2[0]), final_norm, 2)
    return out.reshape(b, l, d)
```

```python
import functools
import math

import jax
import jax.numpy as jnp
from jax import lax
from jax.experimental import pallas as pl
from jax.experimental.pallas import tpu as pltpu

F32 = jnp.float32
BF16 = jnp.bfloat16
EPS = 1e-6

LANES = 128
SUBLANES = 8
VMEM_LIMIT_BYTES = 56 * 1024 * 1024

RET_HEADS = 6
RET_DK = 64
RET_DV = 128
RET_QK = RET_HEADS * RET_DK
RET_V = RET_HEADS * RET_DV
RET_CHUNK = 128
RET_BATCH = 2
ROPE_THETA = 10000.0
S5_GROUP = 16
S5_GROUPS = 16
S5_STATE = 64
S5_WIDTH = S5_GROUP * S5_GROUPS
S5_STEPS = 64
M2_HEADDIM = 64
M2_HEADS = 32
M2_GROUPS = 4
M2_HPG = M2_HEADS // M2_GROUPS
M2_STATE = 128
M2_CONV = 4
M2_CHUNK = 128
M2_DINNER = M2_HEADS * M2_HEADDIM
M2_BC = M2_GROUPS * M2_STATE
XA_HEADS = 4
N_EXPERTS = 8
ROW_TILE = 512


def _params(semantics):
    return pltpu.CompilerParams(dimension_semantics=semantics,
                                vmem_limit_bytes=VMEM_LIMIT_BYTES)


def _rms(x, w):
    return x * lax.rsqrt(jnp.mean(x * x, axis=-1, keepdims=True) + EPS) * w


def _dot(a, b):
    return jnp.dot(a, b, preferred_element_type=F32)


def _split_bf16(x):
    hi = x.astype(BF16)
    lo = (x - hi.astype(F32)).astype(BF16)
    return hi, lo


def _silu(x):
    h = 0.5 * x
    return h + h * jnp.tanh(h)


def _memkv_kernel(mem_ref, nw_ref, wk_ref, wv_ref, k_ref, v_ref):
    m = _rms(mem_ref[0], nw_ref[...]).astype(BF16)
    k_ref[0, 0] = _dot(m, wk_ref[0]).astype(BF16)
    v_ref[0, 0] = _dot(m, wv_ref[0]).astype(BF16)


def _memkv(mem, mem_norm, wk, wv):
    b, nm, d = mem.shape
    depth = wk.shape[0]
    out = jax.ShapeDtypeStruct((depth, b, nm, d), BF16)
    return pl.pallas_call(
        _memkv_kernel,
        out_shape=(out, out),
        grid=(depth, b),
        in_specs=[
            pl.BlockSpec((1, nm, d), lambda l, i: (i, 0, 0)),
            pl.BlockSpec((1, d), lambda l, i: (0, 0)),
            pl.BlockSpec((1, d, d), lambda l, i: (l, 0, 0)),
            pl.BlockSpec((1, d, d), lambda l, i: (l, 0, 0)),
        ],
        out_specs=(pl.BlockSpec((1, 1, nm, d), lambda l, i: (l, i, 0, 0)),
                   pl.BlockSpec((1, 1, nm, d), lambda l, i: (l, i, 0, 0))),
        compiler_params=_params(("arbitrary", "arbitrary")),
        name="memkv",
    )(mem, mem_norm.reshape(1, d), wk, wv)


def _in_proj_kernel(bounds, x_ref, nw_ref, w_ref, *out_refs):
    hn = _rms(x_ref[...], nw_ref[...]).astype(BF16)
    for (lo, hi), o_ref in zip(bounds, out_refs):
        o_ref[...] = _dot(hn, w_ref[:, lo:hi]).astype(o_ref.dtype)


def _in_proj(x, norm_w, w, widths, dtypes):
    n, d = x.shape
    bounds, lo = [], 0
    for wd in widths:
        bounds.append((lo, lo + wd))
        lo += wd
    assert lo == w.shape[1]
    tm = ROW_TILE
    return pl.pallas_call(
        functools.partial(_in_proj_kernel, tuple(bounds)),
        out_shape=tuple(jax.ShapeDtypeStruct((n, wd), dt) for wd, dt in zip(widths, dtypes)),
        grid=(n // tm,),
        in_specs=[
            pl.BlockSpec((tm, d), lambda i: (i, 0)),
            pl.BlockSpec((1, d), lambda i: (0, 0)),
            pl.BlockSpec(w.shape, lambda i: (0, 0)),
        ],
        out_specs=tuple(pl.BlockSpec((tm, wd), lambda i: (i, 0)) for wd in widths),
        compiler_params=_params(("arbitrary",)),
        name="in_proj",
    )(x, norm_w.reshape(1, d), w)


def _ret_kernel(q_ref, k_ref, v_ref, g_ref, pos_ref, invf_ref, rot_ref, din_ref,
                dcr_ref, dst_ref, dch_ref, o_ref, state_ref):
    @pl.when(pl.program_id(1) == 0)
    def _():
        state_ref[...] = jnp.zeros_like(state_ref)

    reps = RET_QK // LANES
    dcr = dcr_ref[...]
    for bi in range(q_ref.shape[0]):
        ang = pos_ref[bi].astype(F32) * invf_ref[...]
        cos = jnp.concatenate([jnp.cos(ang)] * reps, axis=1)
        sin = jnp.concatenate([jnp.sin(ang)] * reps, axis=1)

        def rotary(x, cos=cos, sin=sin):
            hi, lo = _split_bf16(x)
            swapped = _dot(hi, rot_ref[...]) + _dot(lo, rot_ref[...])
            return x * cos + swapped * sin

        q = rotary(q_ref[bi]).astype(BF16)
        kf = rotary(k_ref[bi]) * (RET_DK ** -0.5)
        k = kf.astype(BF16)
        kd = (kf * dst_ref[...]).astype(BF16)
        v = v_ref[bi]
        g = g_ref[bi]
        state = state_ref[bi]
        state_bf = state.astype(BF16)
        for h in range(RET_HEADS):
            ks = slice(h * RET_DK, (h + 1) * RET_DK)
            vs = slice(h * RET_DV, (h + 1) * RET_DV)
            scores = lax.dot_general(q[:, ks], k[:, ks], (((1,), (1,)), ((), ())),
                                     preferred_element_type=F32) * din_ref[h]
            y = _dot(scores.astype(BF16), v[:, vs])
            y = y + _dot(q[:, ks], state_bf[:, vs]) * dcr[:, vs]
            upd = lax.dot_general(kd[:, ks], v[:, vs], (((0,), (0,)), ((), ())),
                                  preferred_element_type=F32)
            state_ref[bi, :, vs] = state[:, vs] * dch_ref[:, vs] + upd
            y = y * lax.rsqrt(jnp.mean(y * y, axis=-1, keepdims=True) + EPS)
            o_ref[bi, :, vs] = (y * _silu(g[:, vs])).astype(o_ref.dtype)


def _retention_tables():
    t = RET_CHUNK
    log_gamma = jnp.log1p(-(2.0 ** (-5.0 - jnp.arange(RET_HEADS, dtype=F32))))
    idx = jnp.arange(t, dtype=F32)
    diff = idx[:, None] - idx[None, :]
    d_inner = jnp.where(diff[None] >= 0,
                        jnp.exp(jnp.maximum(diff, 0.0)[None] * log_gamma[:, None, None]), 0.0)
    d_cross = jnp.exp((idx[:, None] + 1.0) * log_gamma)
    d_state = jnp.exp((t - 1.0 - idx)[:, None] * log_gamma)
    d_chunk = jnp.exp(t * log_gamma)
    d_cross = jnp.repeat(d_cross, RET_DV, axis=1)
    d_state = jnp.repeat(d_state, RET_DK, axis=1)
    d_chunk = jnp.repeat(d_chunk, RET_DV)[None, :]
    half = RET_DK // 2
    inv_freq = ROPE_THETA ** (-jnp.arange(half, dtype=F32) / half)
    inv_freq = jnp.tile(inv_freq, LANES // half)[None, :]
    col = jnp.arange(RET_QK)
    first = (col % RET_DK) < half
    src = jnp.where(first, col + half, col - half)
    rot = jnp.zeros((RET_QK, RET_QK), F32).at[src, col].set(jnp.where(first, -1.0, 1.0))
    return inv_freq, rot.astype(BF16), d_inner, d_cross, d_state, d_chunk


def _retention(q, k, v, g, positions):
    b, l, _ = q.shape
    t = RET_CHUNK
    inv_freq, rot, d_inner, d_cross, d_state, d_chunk = _retention_tables()
    nb = RET_BATCH
    seq = lambda w: pl.BlockSpec((nb, t, w), lambda i, c: (i, c, 0))
    const2 = lambda a: pl.BlockSpec(a.shape, lambda i, c: (0, 0))
    return pl.pallas_call(
        _ret_kernel,
        out_shape=jax.ShapeDtypeStruct((b, l, RET_V), BF16),
        grid=(b // nb, l // t),
        in_specs=[seq(RET_QK), seq(RET_QK), seq(RET_V), seq(RET_V), seq(1),
                  const2(inv_freq), const2(rot),
                  pl.BlockSpec(d_inner.shape, lambda i, c: (0, 0, 0)),
                  const2(d_cross), const2(d_state), const2(d_chunk)],
        out_specs=seq(RET_V),
        scratch_shapes=[pltpu.VMEM((nb, RET_DK, RET_V), F32)],
        compiler_params=_params(("arbitrary", "arbitrary")),
        name="retention",
    )(q, k, v, g, positions.reshape(b, l, 1), inv_freq, rot, d_inner, d_cross, d_state, d_chunk)


def _s5_kernel(u_ref, bin_ref, are_ref, aim_ref, cout_ref, d_ref, wg_ref, bg_ref,
               o_ref, x_scr, st_ref):
    @pl.when(pl.program_id(0) == 0)
    def _():
        st_ref[...] = jnp.zeros_like(st_ref)

    nstate = S5_GROUPS * S5_STATE
    u = u_ref[...]
    x_scr[...] = _dot(u.astype(BF16), bin_ref[...])
    rows = st_ref.shape[1]
    are = jnp.broadcast_to(are_ref[...], (rows, nstate))
    aim = jnp.broadcast_to(aim_ref[...], (rows, nstate))

    def step(t, carry):
        sr, si = carry
        r0 = pl.multiple_of(t * rows, rows)
        xr = x_scr[pl.ds(r0, rows), :nstate]
        xi = x_scr[pl.ds(r0, rows), nstate:]
        nr = are * sr - aim * si + xr
        ni = are * si + aim * sr + xi
        x_scr[pl.ds(r0, rows), :nstate] = nr
        x_scr[pl.ds(r0, rows), nstate:] = ni
        return nr, ni

    sr, si = lax.fori_loop(0, x_scr.shape[0] // rows, step, (st_ref[0], st_ref[1]))
    st_ref[0] = sr
    st_ref[1] = si
    hi, lo = _split_bf16(x_scr[...])
    y = _dot(hi, cout_ref[...]) + _dot(lo, cout_ref[...]) + d_ref[...] * u
    y = jax.nn.gelu(y)
    gate = _dot(y.astype(BF16), wg_ref[...]) + bg_ref[...]
    o_ref[...] = (y * (1.0 / (1.0 + jnp.exp(-gate)))).astype(o_ref.dtype)


def _s5_tables(lam_re, lam_im, log_dt, b_re, b_im, c_re, c_im):
    dt = jnp.exp(log_dt)[:, None]
    mag = jnp.exp(lam_re * dt)
    ab_re, ab_im = mag * jnp.cos(lam_im * dt), mag * jnp.sin(lam_im * dt)
    den = lam_re * lam_re + lam_im * lam_im
    nr, ni = ab_re - 1.0, ab_im
    f_re = (nr * lam_re + ni * lam_im) / den
    f_im = (ni * lam_re - nr * lam_im) / den
    bb_re = f_re[..., None] * b_re - f_im[..., None] * b_im
    bb_im = f_re[..., None] * b_im + f_im[..., None] * b_re
    eye = jnp.eye(S5_GROUPS, dtype=F32)
    bd = lambda m: jnp.einsum('gpc,gh->gchp', m, eye).reshape(S5_WIDTH, S5_GROUPS * S5_STATE)
    b_in = jnp.concatenate([bd(bb_re), bd(bb_im)], axis=1)
    cd = lambda m: jnp.einsum('gcp,gh->gphc', m, eye).reshape(S5_GROUPS * S5_STATE, S5_WIDTH)
    c_out = jnp.concatenate([cd(c_re), -cd(c_im)], axis=0)
    return (b_in.astype(BF16), ab_re.reshape(1, -1), ab_im.reshape(1, -1), c_out.astype(BF16))


def _s5(u_tb, batch, tables, d_skip, w_glu, b_glu):
    rows_total, width = u_tb.shape
    b_in, a_re, a_im, c_out = tables
    nstate = S5_GROUPS * S5_STATE
    tr = S5_STEPS * batch
    const = lambda a: pl.BlockSpec(a.shape, lambda i: (0, 0))
    d_skip = d_skip.reshape(1, width)
    b_glu = b_glu.reshape(1, width)
    return pl.pallas_call(
        _s5_kernel,
        out_shape=jax.ShapeDtypeStruct((rows_total, width), BF16),
        grid=(rows_total // tr,),
        in_specs=[pl.BlockSpec((tr, width), lambda i: (i, 0)),
                  const(b_in), const(a_re), const(a_im), const(c_out),
                  const(d_skip), const(w_glu), const(b_glu)],
        out_specs=pl.BlockSpec((tr, width), lambda i: (i, 0)),
        scratch_shapes=[pltpu.VMEM((tr, 2 * nstate), F32),
                        pltpu.VMEM((2, batch, nstate), F32)],
        compiler_params=_params(("arbitrary",)),
        name="s5",
    )(u_tb, b_in, a_re, a_im, c_out, d_skip, w_glu, b_glu)


def _out_xattn_kernel(n_y, route, *refs):
    res_ref = refs[0]
    y_refs = refs[1:1 + n_y]
    w_refs = refs[1 + n_y:1 + 2 * n_y]
    nx_ref, wq_ref, k_ref, v_ref, wo_ref = refs[1 + 2 * n_y:6 + 2 * n_y]
    o_ref = refs[6 + 2 * n_y + (2 if route else 0)]
    h = res_ref[...]
    for y_ref, w_ref in zip(y_refs, w_refs):
        h = h + _dot(y_ref[...], w_ref[...])
    d = h.shape[1]
    hd = d // XA_HEADS
    q = (_dot(_rms(h, nx_ref[...]).astype(BF16), wq_ref[...]) * (hd ** -0.5)).astype(BF16)
    k = k_ref[0]
    v = v_ref[0]
    outs = []
    for i in range(XA_HEADS):
        hs = slice(i * hd, (i + 1) * hd)
        s = lax.dot_general(q[:, hs], k[:, hs], (((1,), (1,)), ((), ())),
                            preferred_element_type=F32)
        p = jnp.exp(s - jnp.max(s, axis=-1, keepdims=True))
        p = p / jnp.sum(p, axis=-1, keepdims=True)
        outs.append(_dot(p.astype(BF16), v[:, hs]).astype(BF16))
    o = jnp.concatenate(outs, axis=1)
    h = h + _dot(o, wo_ref[...])
    o_ref[...] = h
    if route:
        rnw_ref, r_ref = refs[6 + 2 * n_y:8 + 2 * n_y]
        meta_ref, cnt_ref, run_scr = refs[9 + 2 * n_y:]
        _route(h, rnw_ref, r_ref, meta_ref, cnt_ref, run_scr)


def _out_xattn(res, ys, ws, norm_w, wq, k, v, wo, seq_len, route=None):
    n, d = res.shape
    tm = ROW_TILE
    nm = k.shape[1]
    tiles_per_seq = seq_len // tm
    row = lambda w: pl.BlockSpec((tm, w), lambda i: (i, 0))
    const = lambda a: pl.BlockSpec(a.shape, lambda i: (0, 0))
    kv = pl.BlockSpec((1, nm, d), lambda i: (i // tiles_per_seq, 0, 0))
    args = [res, *ys, *ws, norm_w.reshape(1, d), wq, k, v, wo]
    in_specs = ([row(d)] + [row(y.shape[1]) for y in ys] + [const(w) for w in ws]
                + [pl.BlockSpec((1, d), lambda i: (0, 0)), const(wq), kv, kv, const(wo)])
    out_shape = jax.ShapeDtypeStruct((n, d), F32)
    out_specs = row(d)
    scratch = []
    if route is not None:
        route_norm, router = route
        router_t = router.T
        ne = router_t.shape[0]
        assert ne == SUBLANES, "routing keeps one expert per sublane"
        args += [route_norm.reshape(1, d), router_t]
        in_specs += [pl.BlockSpec((1, d), lambda i: (0, 0)), const(router_t)]
        out_shape = (out_shape, jax.ShapeDtypeStruct((ne, n), F32),
                     jax.ShapeDtypeStruct((ne, LANES), F32))
        out_specs = (out_specs, pl.BlockSpec((ne, tm), lambda i: (0, i)),
                     pl.BlockSpec((ne, LANES), lambda i: (0, 0)))
        scratch = [pltpu.VMEM((ne, LANES), F32)]
    return pl.pallas_call(
        functools.partial(_out_xattn_kernel, len(ys), route is not None),
        out_shape=out_shape,
        grid=(n // tm,),
        in_specs=in_specs,
        out_specs=out_specs,
        scratch_shapes=scratch,
        compiler_params=_params(("arbitrary",)),
        name="out_xattn",
    )(*args)


def _ffn_kernel(h_ref, nw_ref, w1_ref, w3_ref, w2_ref, o_ref, hn_scr):
    f = pl.program_id(1)

    @pl.when(f == 0)
    def _():
        h = h_ref[...]
        hn_scr[...] = _rms(h, nw_ref[...]).astype(BF16)
        o_ref[...] = h

    hn = hn_scr[...]
    a = _dot(hn, w1_ref[...])
    c = _dot(hn, w3_ref[...])
    o_ref[...] += _dot((_silu(a) * c).astype(BF16), w2_ref[...])


def _ffn(h, norm_w, w1, w3, w2, n_chunks):
    n, d = h.shape
    f = w1.shape[1]
    fc = f // n_chunks
    tm = ROW_TILE
    return pl.pallas_call(
        _ffn_kernel,
        out_shape=jax.ShapeDtypeStruct((n, d), F32),
        grid=(n // tm, n_chunks),
        in_specs=[pl.BlockSpec((tm, d), lambda i, j: (i, 0)),
                  pl.BlockSpec((1, d), lambda i, j: (0, 0)),
                  pl.BlockSpec((d, fc), lambda i, j: (0, j)),
                  pl.BlockSpec((d, fc), lambda i, j: (0, j)),
                  pl.BlockSpec((fc, d), lambda i, j: (j, 0))],
        out_specs=pl.BlockSpec((tm, d), lambda i, j: (i, 0)),
        scratch_shapes=[pltpu.VMEM((tm, d), BF16)],
        compiler_params=_params(("arbitrary", "arbitrary")),
        name="ffn",
    )(h, norm_w.reshape(1, d), w1, w3, w2)


def _softplus(x):
    return jnp.maximum(x, 0.0) + jnp.log1p(jnp.exp(-jnp.abs(x)))


def _m2_in_kernel(tiles_per_seq, x_ref, nw_ref, w_ref, cw_ref, cb_ref, zs_ref, xc_ref, dt_ref,
                  tail_scr):
    tm = x_ref.shape[0]
    pad = SUBLANES
    cdim = xc_ref.shape[1]
    hn = _rms(x_ref[...], nw_ref[...]).astype(BF16)

    @pl.when(pl.program_id(0) % tiles_per_seq == 0)
    def _():
        tail_scr[...] = jnp.zeros_like(tail_scr)

    blk = 4 * LANES
    for c0 in range(0, cdim, blk):
        cs = slice(c0, c0 + blk)
        xbc = _dot(hn, w_ref[:, M2_DINNER + c0:M2_DINNER + c0 + blk])
        if c0 < M2_DINNER:
            zs_ref[:, cs] = _silu(_dot(hn, w_ref[:, cs])).astype(zs_ref.dtype)
        ext = jnp.concatenate([tail_scr[:, cs], xbc], axis=0)
        acc = cb_ref[:, cs] + cw_ref[M2_CONV - 1:M2_CONV, cs] * xbc
        for j in range(M2_CONV - 1):
            shifted = pltpu.roll(ext, M2_CONV - 1 - j, 0)[pad:pad + tm, :]
            acc = acc + cw_ref[j:j + 1, cs] * shifted
        xc_ref[:, cs] = _silu(acc).astype(xc_ref.dtype)
        tail_scr[:, cs] = xbc[tm - pad:, :]
    dt_ref[...] = _dot(hn, w_ref[:, M2_DINNER + cdim:])


def _m2_in_proj(x, norm_w, w, conv_w, conv_b, seq_len):
    n, d = x.shape
    cdim = conv_w.shape[1]
    tm = ROW_TILE
    row = lambda wd: pl.BlockSpec((tm, wd), lambda i: (i, 0))
    const = lambda a: pl.BlockSpec(a.shape, lambda i: (0, 0))
    conv_b = conv_b.reshape(1, cdim)
    return pl.pallas_call(
        functools.partial(_m2_in_kernel, seq_len // tm),
        out_shape=(jax.ShapeDtypeStruct((n, M2_DINNER), BF16),
                   jax.ShapeDtypeStruct((n, cdim), BF16),
                   jax.ShapeDtypeStruct((n, M2_HEADS), F32)),
        grid=(n // tm,),
        in_specs=[row(d), pl.BlockSpec((1, d), lambda i: (0, 0)), const(w), const(conv_w),
                  const(conv_b)],
        out_specs=(row(M2_DINNER), row(cdim), row(M2_HEADS)),
        scratch_shapes=[pltpu.VMEM((SUBLANES, cdim), F32)],
        compiler_params=_params(("arbitrary",)),
        name="m2_in_proj",
    )(x, norm_w.reshape(1, d), w, conv_w, conv_b)


def _ssd_kernel(zs_ref, xc_ref, dt_ref, dtt_ref, dtb_ref, dtbt_ref, alog_ref, alogt_ref,
                dsk_ref, nw_ref, o_ref, state_scr, y_scr):
    t = M2_CHUNK

    @pl.when(pl.program_id(1) == 0)
    def _():
        state_scr[...] = jnp.zeros_like(state_scr)

    xc = xc_ref[0]
    xs_bf = xc[:, :M2_DINNER]
    bm = xc[:, M2_DINNER:M2_DINNER + M2_BC]
    cm = xc[:, M2_DINNER + M2_BC:]

    dt = _softplus(dt_ref[0] + dtb_ref[...])
    adt = dt * (-jnp.exp(alog_ref[...]))
    dtt = _softplus(dtt_ref[0] + dtbt_ref[...])
    adtt = dtt * (-jnp.exp(alogt_ref[...]))
    ri = lax.broadcasted_iota(jnp.int32, (t, t), 0)
    ci = lax.broadcasted_iota(jnp.int32, (t, t), 1)
    causal = ri >= ci
    tri = causal.astype(F32)
    acum = jnp.dot(tri, adt, preferred_element_type=F32, precision=lax.Precision.HIGHEST)
    acumt = lax.dot_general(adtt, tri, (((1,), (1,)), ((), ())), preferred_element_type=F32,
                            precision=lax.Precision.HIGHEST)
    arow = acumt - jnp.log(dtt)
    alast = acum[t - 1:t, :]
    wdec = (dt * jnp.exp(alast - acum)).astype(BF16)
    eac = jnp.exp(acum).astype(BF16)
    sdec = jnp.exp(alast)

    state = state_scr[...]
    state_bf = state.astype(BF16)
    lane = lax.broadcasted_iota(jnp.int32, (t, M2_DINNER), 1)
    even = (lane % (2 * M2_HEADDIM)) < M2_HEADDIM
    zero = jnp.zeros((), BF16)
    rhs_half = (jnp.concatenate([jnp.where(even, xs_bf, zero), jnp.where(even, state_bf, zero)], axis=0),
                jnp.concatenate([jnp.where(even, zero, xs_bf), jnp.where(even, zero, state_bf)], axis=0))
    for g in range(M2_GROUPS):
        ns = slice(g * M2_STATE, (g + 1) * M2_STATE)
        cb = lax.dot_general(cm[:, ns], bm[:, ns], (((1,), (1,)), ((), ())),
                             preferred_element_type=F32).astype(BF16)
        wcols, scols = [], []
        for kp in range(M2_HPG // 2):
            pair = slice((g * M2_HPG + 2 * kp) * M2_HEADDIM, (g * M2_HPG + 2 * kp + 2) * M2_HEADDIM)
            y = None
            for half in range(2):
                h = g * M2_HPG + 2 * kp + half
                seg = acum[:, h:h + 1] - arow[h:h + 1, :]
                lmat = jnp.exp(jnp.where(causal, seg, -jnp.inf)).astype(BF16)
                coff = cm[:, ns] * eac[:, h:h + 1]
                lhs = jnp.concatenate([cb * lmat, coff], axis=1)
                part = _dot(lhs, rhs_half[half][:, pair])
                y = part if y is None else y + part
                wcols.append(jnp.broadcast_to(wdec[:, h:h + 1], (t, M2_HEADDIM)))
                scols.append(jnp.broadcast_to(sdec[:, h:h + 1], (1, M2_HEADDIM)))
            y_scr[:, pair] = y
        gs = slice(g * M2_HPG * M2_HEADDIM, (g + 1) * M2_HPG * M2_HEADDIM)
        xd = xs_bf[:, gs] * jnp.concatenate(wcols, axis=1)
        upd = lax.dot_general(bm[:, ns], xd, (((0,), (0,)), ((), ())),
                              preferred_element_type=F32)
        state_scr[:, gs] = state[:, gs] * jnp.concatenate(scols, axis=1) + upd

    yz = (y_scr[...] + dsk_ref[...] * xs_bf.astype(F32)) * zs_ref[0].astype(F32)
    o_ref[0] = _rms(yz, nw_ref[...]).astype(o_ref.dtype)


def _ssd(zs, xc, dt, dt_bias, a_log, d_skip, norm_w):
    b, l, _ = zs.shape
    t = M2_CHUNK
    cdim = xc.shape[2]
    dtt = jnp.swapaxes(dt, 1, 2)
    row = lambda a: a.reshape(1, -1)
    col = lambda a: a.reshape(-1, 1)
    seq = lambda w: pl.BlockSpec((1, t, w), lambda i, c: (i, c, 0))
    const = lambda a: pl.BlockSpec(a.shape, lambda i, c: (0, 0))
    args = [zs, xc, dt, dtt, row(dt_bias), col(dt_bias), row(a_log), col(a_log),
            row(jnp.repeat(d_skip, M2_HEADDIM)), row(norm_w)]
    in_specs = [seq(M2_DINNER), seq(cdim), seq(M2_HEADS),
                pl.BlockSpec((1, M2_HEADS, t), lambda i, c: (i, 0, c))]
    in_specs += [const(a) for a in args[4:]]
    return pl.pallas_call(
        _ssd_kernel,
        out_shape=jax.ShapeDtypeStruct((b, l, M2_DINNER), BF16),
        grid=(b, l // t),
        in_specs=in_specs,
        out_specs=seq(M2_DINNER),
        scratch_shapes=[pltpu.VMEM((M2_STATE, M2_DINNER), F32),
                        pltpu.VMEM((t, M2_DINNER), F32)],
        compiler_params=_params(("arbitrary", "arbitrary")),
        name="ssd",
    )(*args)


MOE_TILE = 512
META_I1, META_I2, META_R1, META_R2, META_G1, META_G2 = range(6)


def _route(h, nw_ref, r_ref, meta_ref, cnt_ref, run_scr):
    @pl.when(pl.program_id(0) == 0)
    def _():
        run_scr[...] = jnp.zeros_like(run_scr)

    tm = h.shape[0]
    ne = r_ref.shape[0]
    eidx = lax.broadcasted_iota(jnp.int32, (ne, tm), 0).astype(F32)
    nt = lambda a, b: lax.dot_general(a, b, (((1,), (1,)), ((), ())), preferred_element_type=F32)
    hn_hi, hn_lo = _split_bf16(_rms(h, nw_ref[...]))
    r_hi, r_lo = _split_bf16(r_ref[...])
    logits = nt(r_hi, hn_hi) + nt(r_lo, hn_hi) + nt(r_hi, hn_lo)
    m1 = jnp.max(logits, axis=0, keepdims=True)
    i1 = jnp.min(jnp.where(logits == m1, eidx, float(ne)), axis=0, keepdims=True)
    rest = jnp.where(eidx == i1, -jnp.inf, logits)
    m2 = jnp.max(rest, axis=0, keepdims=True)
    i2 = jnp.min(jnp.where(rest == m2, eidx, float(ne)), axis=0, keepdims=True)
    e2 = jnp.exp(m2 - m1)
    g1 = 1.0 / (1.0 + e2)
    g2 = e2 / (1.0 + e2)
    oh1 = (eidx == i1).astype(F32)
    oh2 = (eidx == i2).astype(F32)
    oh = oh1 + oh2
    ri = lax.broadcasted_iota(jnp.int32, (tm, tm), 0)
    ci = lax.broadcasted_iota(jnp.int32, (tm, tm), 1)
    before = _dot(oh.astype(BF16), (ri < ci).astype(BF16)) + run_scr[:, 0:1]
    r1 = jnp.sum(oh1 * before, axis=0, keepdims=True)
    r2 = jnp.sum(oh2 * before, axis=0, keepdims=True)
    run_scr[...] = run_scr[...] + jnp.sum(oh, axis=1, keepdims=True)
    cnt_ref[...] = run_scr[...]
    meta = jnp.zeros((ne, tm), F32)
    for row, val in ((META_I1, i1), (META_I2, i2), (META_R1, r1), (META_R2, r2),
                     (META_G1, g1), (META_G2, g2)):
        meta = jnp.where(eidx == float(row), val, meta)
    meta_ref[...] = meta


def _row_copy(src_hbm, src_row, dst_ref, dst_row, sem):
    return pltpu.make_async_copy(src_hbm.at[pl.ds(src_row, 1)], dst_ref.at[pl.ds(dst_row, 1)], sem)


def _gmm_kernel(n_chunks, te_ref, tv_ref, src_ref, nxt_ref, hn_hbm, nw_ref, w1_ref, w3_ref, w2_ref,
                o_ref, xg_scr, sems):
    del te_ref
    j = pl.program_id(0)
    tg = xg_scr.shape[1]
    fc = w1_ref.shape[2] // n_chunks
    slot = lax.rem(j, 2)

    @pl.when(j == 0)
    def _():
        def issue(r, carry):
            _row_copy(hn_hbm, src_ref[0, 0, r], xg_scr.at[0], r, sems.at[0]).start()
            return carry
        lax.fori_loop(0, tg, issue, 0, unroll=8)

    @pl.when(jnp.logical_or(j == 0, tv_ref[jnp.maximum(j - 1, 0)] != 0))
    def _():
        def drain(r, carry):
            _row_copy(hn_hbm, 0, xg_scr.at[slot], 0, sems.at[slot]).wait()
            return carry
        lax.fori_loop(0, tg, drain, 0, unroll=8)

    @pl.when(tv_ref[j] != 0)
    def _():
        for r in range(tg):
            _row_copy(hn_hbm, nxt_ref[0, 0, r], xg_scr.at[1 - slot], r, sems.at[1 - slot]).start()
        x = _rms(xg_scr[slot], nw_ref[...]).astype(BF16)
        for c in range(n_chunks):
            fs = slice(c * fc, (c + 1) * fc)
            a = _dot(x, w1_ref[0, :, fs])
            b = _dot(x, w3_ref[0, :, fs])
            part = _dot((_silu(a) * b).astype(BF16), w2_ref[0, fs, :])
            if c == 0:
                o_ref[...] = part
            else:
                o_ref[...] += part

    @pl.when(tv_ref[j] == 0)
    def _():
        o_ref[...] = jnp.zeros_like(o_ref)


def _grouped_swiglu(hn, norm_w, src, tile_expert, tile_valid, w1, w3, w2, n_chunks):
    d = hn.shape[1]
    f = w1.shape[2]
    tg = MOE_TILE
    n_tiles = src.shape[0] // tg
    src = src.reshape(n_tiles, 1, tg)
    expert = lambda shape: pl.BlockSpec(shape, lambda j, te, tv: (te[j], 0, 0),
                                        pipeline_mode=pl.Buffered(1))
    grid_spec = pltpu.PrefetchScalarGridSpec(
        num_scalar_prefetch=2,
        grid=(n_tiles,),
        in_specs=[pl.BlockSpec((1, 1, tg), lambda j, te, tv: (j, 0, 0), memory_space=pltpu.SMEM),
                  pl.BlockSpec((1, 1, tg), lambda j, te, tv: (jnp.minimum(j + 1, n_tiles - 1), 0, 0),
                               memory_space=pltpu.SMEM),
                  pl.BlockSpec(memory_space=pl.ANY),
                  pl.BlockSpec((1, d), lambda j, te, tv: (0, 0)),
                  expert((1, d, f)), expert((1, d, f)), expert((1, f, d))],
        out_specs=pl.BlockSpec((tg, d), lambda j, te, tv: (j, 0)),
        scratch_shapes=[pltpu.VMEM((2, tg, d), F32), pltpu.SemaphoreType.DMA((2,))],
    )
    return pl.pallas_call(
        functools.partial(_gmm_kernel, n_chunks),
        out_shape=jax.ShapeDtypeStruct((n_tiles * tg, d), F32),
        grid_spec=grid_spec,
        compiler_params=_params(("arbitrary",)),
        name="moe_gmm",
    )(tile_expert, tile_valid, src, src, hn, norm_w.reshape(1, d), w1, w3, w2)


def _combine_kernel(pos1_ref, pos2_ref, h_ref, meta_ref, fw_ref, ys_hbm, o_ref,
                    y1_scr, y2_scr, sems):
    rows = h_ref.shape[0]
    for t in range(rows):
        _row_copy(ys_hbm, pos1_ref[0, 0, t], y1_scr, t, sems.at[0]).start()
        _row_copy(ys_hbm, pos2_ref[0, 0, t], y2_scr, t, sems.at[1]).start()
    for t in range(rows):
        _row_copy(ys_hbm, 0, y1_scr, t, sems.at[0]).wait()
        _row_copy(ys_hbm, 0, y2_scr, t, sems.at[1]).wait()
    gates = meta_ref[...]
    o_ref[...] = _rms(h_ref[...] + gates[:, 0:1] * y1_scr[...] + gates[:, 1:2] * y2_scr[...],
                      fw_ref[...])


def _combine(h, meta, pos1, pos2, ys, final_w):
    n, d = h.shape
    tm = ROW_TILE
    steps = n // tm
    idx = lambda: pl.BlockSpec((1, 1, tm), lambda i: (i, 0, 0), memory_space=pltpu.SMEM)
    return pl.pallas_call(
        _combine_kernel,
        out_shape=jax.ShapeDtypeStruct((n, d), F32),
        grid=(steps,),
        in_specs=[idx(), idx(),
                  pl.BlockSpec((tm, d), lambda i: (i, 0)),
                  pl.BlockSpec((tm, meta.shape[1]), lambda i: (i, 0)),
                  pl.BlockSpec((1, d), lambda i: (0, 0)),
                  pl.BlockSpec(memory_space=pl.ANY)],
        out_specs=pl.BlockSpec((tm, d), lambda i: (i, 0)),
        scratch_shapes=[pltpu.VMEM((tm, d), F32), pltpu.VMEM((tm, d), F32),
                        pltpu.SemaphoreType.DMA((2,))],
        compiler_params=_params(("arbitrary",)),
        name="moe_combine",
    )(pos1.reshape(steps, 1, tm), pos2.reshape(steps, 1, tm), h, meta,
      final_w.reshape(1, d), ys)


def _moe(h, meta, counts, norm_w, w1, w3, w2, final_w, n_chunks):
    n, d = h.shape
    ne = w1.shape[0]
    tg = MOE_TILE
    counts = counts[:, 0].astype(jnp.int32)
    padded = ((counts + tg - 1) // tg) * tg
    ends = jnp.cumsum(padded)
    offs = ends - padded
    col = lambda c: meta[c].astype(jnp.int32)
    pos1 = offs[col(META_I1)] + col(META_R1)
    pos2 = offs[col(META_I2)] + col(META_R2)
    n_tiles = (2 * n) // tg + ne + 1
    starts = jnp.arange(n_tiles, dtype=jnp.int32) * tg
    tile_expert = jnp.minimum(jnp.searchsorted(ends, starts, side='right'), ne - 1).astype(jnp.int32)
    tile_valid = (starts < ends[-1]).astype(jnp.int32)
    tok = jnp.arange(n, dtype=jnp.int32)
    src = jnp.zeros((n_tiles * tg,), jnp.int32).at[jnp.concatenate([pos1, pos2])].set(
        jnp.concatenate([tok, tok]), unique_indices=True)
    ys = _grouped_swiglu(h, norm_w, src, tile_expert, tile_valid, w1, w3, w2, n_chunks)
    gates = meta[META_G1:META_G2 + 1].T
    return _combine(h, gates, pos1, pos2, ys, final_w)


def kernel(x, mem, positions, mem_norm, norm_mix, norm_xattn, norm_ffn, xa_wq, xa_wk, xa_wv, xa_wo, ev_w_in, ev_s5_lam_re, ev_s5_lam_im, ev_s5_log_dt, ev_s5_b_re, ev_s5_b_im, ev_s5_c_re, ev_s5_c_im, ev_s5_d, ev_s5_w_glu, ev_s5_b_glu, ev_w_out, ev_ffn_w1, ev_ffn_w3, ev_ffn_w2, od_w_in, od_conv_w, od_conv_b, od_dt_bias, od_a_log, od_d, od_norm, od_w_out, od_router, od_moe_w1, od_moe_w3, od_moe_w2, final_norm):
    b, l, d = x.shape
    n = b * l
    assert l % ROW_TILE == 0 and l % RET_CHUNK == 0 and l % M2_CHUNK == 0 and l % S5_STEPS == 0
    assert b == SUBLANES, "the S5 scan maps the batch onto the sublanes"
    bf = lambda a: a.astype(BF16)

    mem_k, mem_v = _memkv(mem, mem_norm, bf(xa_wk), bf(xa_wv))
    h = x.reshape(n, d)

    q, k, v, g, u = _in_proj(
        h, norm_mix[0], bf(ev_w_in[0]),
        (RET_QK, RET_QK, RET_V, RET_V, S5_WIDTH), (F32, F32, BF16, F32, F32))
    y_ret = _retention(q.reshape(b, l, -1), k.reshape(b, l, -1), v.reshape(b, l, -1),
                       g.reshape(b, l, -1), positions)
    u_tb = jnp.swapaxes(u.reshape(b, l, S5_WIDTH), 0, 1).reshape(n, S5_WIDTH)
    tables = _s5_tables(ev_s5_lam_re[0], ev_s5_lam_im[0], ev_s5_log_dt[0], ev_s5_b_re[0],
                        ev_s5_b_im[0], ev_s5_c_re[0], ev_s5_c_im[0])
    y_s5 = _s5(u_tb, b, tables, ev_s5_d[0], bf(ev_s5_w_glu[0]), ev_s5_b_glu[0])
    y_s5 = jnp.swapaxes(y_s5.reshape(l, b, S5_WIDTH), 0, 1).reshape(n, S5_WIDTH)
    w_out = bf(ev_w_out[0])
    h = _out_xattn(h, [y_ret.reshape(n, RET_V), y_s5], [w_out[:RET_V], w_out[RET_V:]],
                   norm_xattn[0], bf(xa_wq[0]), mem_k[0], mem_v[0], bf(xa_wo[0]), l)
    h = _ffn(h, norm_ffn[0], bf(ev_ffn_w1[0]), bf(ev_ffn_w3[0]), bf(ev_ffn_w2[0]), 2)

    zs, xc, dt = _m2_in_proj(h, norm_mix[1], bf(od_w_in[0]), od_conv_w[0], od_conv_b[0], l)
    y = _ssd(zs.reshape(b, l, -1), xc.reshape(b, l, -1), dt.reshape(b, l, -1),
             od_dt_bias[0], od_a_log[0], od_d[0], od_norm[0])
    h, meta, counts = _out_xattn(h, [y.reshape(n, M2_DINNER)], [bf(od_w_out[0])],
                                 norm_xattn[1], bf(xa_wq[1]), mem_k[1], mem_v[1], bf(xa_wo[1]), l,
                                 route=(norm_ffn[1], od_router[0]))
    out = _moe(h, meta, counts, norm_ffn[1], bf(od_moe_w1[0]), bf(od_moe_w3[0]),
               bf(od_moe_w2[0]), final_norm, 2)
    return out.reshape(b, l, d)
```

```python
import functools
import math

import jax
import jax.numpy as jnp
from jax import lax
from jax.experimental import pallas as pl
from jax.experimental.pallas import tpu as pltpu

F32 = jnp.float32
BF16 = jnp.bfloat16
EPS = 1e-6

LANES = 128
SUBLANES = 8
VMEM_LIMIT_BYTES = 56 * 1024 * 1024

RET_HEADS = 6
RET_DK = 64
RET_DV = 128
RET_QK = RET_HEADS * RET_DK
RET_V = RET_HEADS * RET_DV
RET_CHUNK = 256
RET_BATCH = 2
ROPE_THETA = 10000.0
S5_GROUP = 16
S5_GROUPS = 16
S5_STATE = 64
S5_WIDTH = S5_GROUP * S5_GROUPS
S5_STEPS = 64
M2_HEADDIM = 64
M2_HEADS = 32
M2_GROUPS = 4
M2_HPG = M2_HEADS // M2_GROUPS
M2_STATE = 128
M2_CONV = 4
M2_CHUNK = 128
M2_DINNER = M2_HEADS * M2_HEADDIM
M2_BC = M2_GROUPS * M2_STATE
XA_HEADS = 4
N_EXPERTS = 8
ROW_TILE = 512


def _params(semantics):
    return pltpu.CompilerParams(dimension_semantics=semantics,
                                vmem_limit_bytes=VMEM_LIMIT_BYTES)


def _rms(x, w):
    return x * lax.rsqrt(jnp.mean(x * x, axis=-1, keepdims=True) + EPS) * w


def _dot(a, b):
    return jnp.dot(a, b, preferred_element_type=F32)


def _split_bf16(x):
    hi = x.astype(BF16)
    lo = (x - hi.astype(F32)).astype(BF16)
    return hi, lo


def _silu(x):
    h = 0.5 * x
    return h + h * jnp.tanh(h)


def _memkv_kernel(mem_ref, nw_ref, wk_ref, wv_ref, k_ref, v_ref):
    m = _rms(mem_ref[0], nw_ref[...]).astype(BF16)
    k_ref[0, 0] = _dot(m, wk_ref[0]).astype(BF16)
    v_ref[0, 0] = _dot(m, wv_ref[0]).astype(BF16)


def _memkv(mem, mem_norm, wk, wv):
    b, nm, d = mem.shape
    depth = wk.shape[0]
    out = jax.ShapeDtypeStruct((depth, b, nm, d), BF16)
    return pl.pallas_call(
        _memkv_kernel,
        out_shape=(out, out),
        grid=(depth, b),
        in_specs=[
            pl.BlockSpec((1, nm, d), lambda l, i: (i, 0, 0)),
            pl.BlockSpec((1, d), lambda l, i: (0, 0)),
            pl.BlockSpec((1, d, d), lambda l, i: (l, 0, 0)),
            pl.BlockSpec((1, d, d), lambda l, i: (l, 0, 0)),
        ],
        out_specs=(pl.BlockSpec((1, 1, nm, d), lambda l, i: (l, i, 0, 0)),
                   pl.BlockSpec((1, 1, nm, d), lambda l, i: (l, i, 0, 0))),
        compiler_params=_params(("arbitrary", "arbitrary")),
        name="memkv",
    )(mem, mem_norm.reshape(1, d), wk, wv)


def _in_proj_kernel(bounds, x_ref, nw_ref, w_ref, *out_refs):
    hn = _rms(x_ref[...], nw_ref[...]).astype(BF16)
    for (lo, hi), o_ref in zip(bounds, out_refs):
        o_ref[...] = _dot(hn, w_ref[:, lo:hi]).astype(o_ref.dtype)


def _in_proj(x, norm_w, w, widths, dtypes):
    n, d = x.shape
    bounds, lo = [], 0
    for wd in widths:
        bounds.append((lo, lo + wd))
        lo += wd
    assert lo == w.shape[1]
    tm = ROW_TILE
    return pl.pallas_call(
        functools.partial(_in_proj_kernel, tuple(bounds)),
        out_shape=tuple(jax.ShapeDtypeStruct((n, wd), dt) for wd, dt in zip(widths, dtypes)),
        grid=(n // tm,),
        in_specs=[
            pl.BlockSpec((tm, d), lambda i: (i, 0)),
            pl.BlockSpec((1, d), lambda i: (0, 0)),
            pl.BlockSpec(w.shape, lambda i: (0, 0)),
        ],
        out_specs=tuple(pl.BlockSpec((tm, wd), lambda i: (i, 0)) for wd in widths),
        compiler_params=_params(("arbitrary",)),
        name="in_proj",
    )(x, norm_w.reshape(1, d), w)


def _ret_kernel(q_ref, k_ref, v_ref, g_ref, pos_ref, invf_ref, rot_ref, din_ref,
                dcr_ref, dst_ref, dch_ref, o_ref, state_ref):
    @pl.when(pl.program_id(1) == 0)
    def _():
        state_ref[...] = jnp.zeros_like(state_ref)

    reps = RET_QK // LANES
    dcr = dcr_ref[...]
    for bi in range(q_ref.shape[0]):
        ang = pos_ref[bi].astype(F32) * invf_ref[...]
        cos = jnp.concatenate([jnp.cos(ang)] * reps, axis=1)
        sin = jnp.concatenate([jnp.sin(ang)] * reps, axis=1)

        def rotary(x, cos=cos, sin=sin):
            hi, lo = _split_bf16(x)
            swapped = _dot(hi, rot_ref[...]) + _dot(lo, rot_ref[...])
            return x * cos + swapped * sin

        q = rotary(q_ref[bi]).astype(BF16)
        kf = rotary(k_ref[bi]) * (RET_DK ** -0.5)
        k = kf.astype(BF16)
        kd = (kf * dst_ref[...]).astype(BF16)
        v = v_ref[bi]
        g = g_ref[bi]
        state = state_ref[bi]
        state_bf = state.astype(BF16)
        for h in range(RET_HEADS):
            ks = slice(h * RET_DK, (h + 1) * RET_DK)
            vs = slice(h * RET_DV, (h + 1) * RET_DV)
            scores = lax.dot_general(q[:, ks], k[:, ks], (((1,), (1,)), ((), ())),
                                     preferred_element_type=F32) * din_ref[h]
            y = _dot(scores.astype(BF16), v[:, vs])
            y = y + _dot(q[:, ks], state_bf[:, vs]) * dcr[:, vs]
            upd = lax.dot_general(kd[:, ks], v[:, vs], (((0,), (0,)), ((), ())),
                                  preferred_element_type=F32)
            state_ref[bi, :, vs] = state[:, vs] * dch_ref[:, vs] + upd
            y = y * lax.rsqrt(jnp.mean(y * y, axis=-1, keepdims=True) + EPS)
            o_ref[bi, :, vs] = (y * _silu(g[:, vs])).astype(o_ref.dtype)


def _retention_tables():
    t = RET_CHUNK
    log_gamma = jnp.log1p(-(2.0 ** (-5.0 - jnp.arange(RET_HEADS, dtype=F32))))
    idx = jnp.arange(t, dtype=F32)
    diff = idx[:, None] - idx[None, :]
    d_inner = jnp.where(diff[None] >= 0,
                        jnp.exp(jnp.maximum(diff, 0.0)[None] * log_gamma[:, None, None]), 0.0)
    d_cross = jnp.exp((idx[:, None] + 1.0) * log_gamma)
    d_state = jnp.exp((t - 1.0 - idx)[:, None] * log_gamma)
    d_chunk = jnp.exp(t * log_gamma)
    d_cross = jnp.repeat(d_cross, RET_DV, axis=1)
    d_state = jnp.repeat(d_state, RET_DK, axis=1)
    d_chunk = jnp.repeat(d_chunk, RET_DV)[None, :]
    half = RET_DK // 2
    inv_freq = ROPE_THETA ** (-jnp.arange(half, dtype=F32) / half)
    inv_freq = jnp.tile(inv_freq, LANES // half)[None, :]
    col = jnp.arange(RET_QK)
    first = (col % RET_DK) < half
    src = jnp.where(first, col + half, col - half)
    rot = jnp.zeros((RET_QK, RET_QK), F32).at[src, col].set(jnp.where(first, -1.0, 1.0))
    return inv_freq, rot.astype(BF16), d_inner, d_cross, d_state, d_chunk


def _retention(q, k, v, g, positions):
    b, l, _ = q.shape
    t = RET_CHUNK
    inv_freq, rot, d_inner, d_cross, d_state, d_chunk = _retention_tables()
    nb = RET_BATCH
    seq = lambda w: pl.BlockSpec((nb, t, w), lambda i, c: (i, c, 0))
    const2 = lambda a: pl.BlockSpec(a.shape, lambda i, c: (0, 0))
    return pl.pallas_call(
        _ret_kernel,
        out_shape=jax.ShapeDtypeStruct((b, l, RET_V), BF16),
        grid=(b // nb, l // t),
        in_specs=[seq(RET_QK), seq(RET_QK), seq(RET_V), seq(RET_V), seq(1),
                  const2(inv_freq), const2(rot),
                  pl.BlockSpec(d_inner.shape, lambda i, c: (0, 0, 0)),
                  const2(d_cross), const2(d_state), const2(d_chunk)],
        out_specs=seq(RET_V),
        scratch_shapes=[pltpu.VMEM((nb, RET_DK, RET_V), F32)],
        compiler_params=_params(("arbitrary", "arbitrary")),
        name="retention",
    )(q, k, v, g, positions.reshape(b, l, 1), inv_freq, rot, d_inner, d_cross, d_state, d_chunk)


def _s5_kernel(u_ref, bin_ref, are_ref, aim_ref, cout_ref, d_ref, wg_ref, bg_ref,
               o_ref, x_scr, st_ref):
    @pl.when(pl.program_id(0) == 0)
    def _():
        st_ref[...] = jnp.zeros_like(st_ref)

    nstate = S5_GROUPS * S5_STATE
    u = u_ref[...]
    x_scr[...] = _dot(u.astype(BF16), bin_ref[...])
    rows = st_ref.shape[1]
    are = jnp.broadcast_to(are_ref[...], (rows, nstate))
    aim = jnp.broadcast_to(aim_ref[...], (rows, nstate))

    def step(t, carry):
        sr, si = carry
        r0 = pl.multiple_of(t * rows, rows)
        xr = x_scr[pl.ds(r0, rows), :nstate]
        xi = x_scr[pl.ds(r0, rows), nstate:]
        nr = are * sr - aim * si + xr
        ni = are * si + aim * sr + xi
        x_scr[pl.ds(r0, rows), :nstate] = nr
        x_scr[pl.ds(r0, rows), nstate:] = ni
        return nr, ni

    sr, si = lax.fori_loop(0, x_scr.shape[0] // rows, step, (st_ref[0], st_ref[1]))
    st_ref[0] = sr
    st_ref[1] = si
    hi, lo = _split_bf16(x_scr[...])
    y = _dot(hi, cout_ref[...]) + _dot(lo, cout_ref[...]) + d_ref[...] * u
    y = jax.nn.gelu(y)
    gate = _dot(y.astype(BF16), wg_ref[...]) + bg_ref[...]
    o_ref[...] = (y * (1.0 / (1.0 + jnp.exp(-gate)))).astype(o_ref.dtype)


def _s5_tables(lam_re, lam_im, log_dt, b_re, b_im, c_re, c_im):
    dt = jnp.exp(log_dt)[:, None]
    mag = jnp.exp(lam_re * dt)
    ab_re, ab_im = mag * jnp.cos(lam_im * dt), mag * jnp.sin(lam_im * dt)
    den = lam_re * lam_re + lam_im * lam_im
    nr, ni = ab_re - 1.0, ab_im
    f_re = (nr * lam_re + ni * lam_im) / den
    f_im = (ni * lam_re - nr * lam_im) / den
    bb_re = f_re[..., None] * b_re - f_im[..., None] * b_im
    bb_im = f_re[..., None] * b_im + f_im[..., None] * b_re
    eye = jnp.eye(S5_GROUPS, dtype=F32)
    bd = lambda m: jnp.einsum('gpc,gh->gchp', m, eye).reshape(S5_WIDTH, S5_GROUPS * S5_STATE)
    b_in = jnp.concatenate([bd(bb_re), bd(bb_im)], axis=1)
    cd = lambda m: jnp.einsum('gcp,gh->gphc', m, eye).reshape(S5_GROUPS * S5_STATE, S5_WIDTH)
    c_out = jnp.concatenate([cd(c_re), -cd(c_im)], axis=0)
    return (b_in.astype(BF16), ab_re.reshape(1, -1), ab_im.reshape(1, -1), c_out.astype(BF16))


def _s5(u_tb, batch, tables, d_skip, w_glu, b_glu):
    rows_total, width = u_tb.shape
    b_in, a_re, a_im, c_out = tables
    nstate = S5_GROUPS * S5_STATE
    tr = S5_STEPS * batch
    const = lambda a: pl.BlockSpec(a.shape, lambda i: (0, 0))
    d_skip = d_skip.reshape(1, width)
    b_glu = b_glu.reshape(1, width)
    return pl.pallas_call(
        _s5_kernel,
        out_shape=jax.ShapeDtypeStruct((rows_total, width), BF16),
        grid=(rows_total // tr,),
        in_specs=[pl.BlockSpec((tr, width), lambda i: (i, 0)),
                  const(b_in), const(a_re), const(a_im), const(c_out),
                  const(d_skip), const(w_glu), const(b_glu)],
        out_specs=pl.BlockSpec((tr, width), lambda i: (i, 0)),
        scratch_shapes=[pltpu.VMEM((tr, 2 * nstate), F32),
                        pltpu.VMEM((2, batch, nstate), F32)],
        compiler_params=_params(("arbitrary",)),
        name="s5",
    )(u_tb, b_in, a_re, a_im, c_out, d_skip, w_glu, b_glu)


def _out_xattn_kernel(n_y, route, *refs):
    res_ref = refs[0]
    y_refs = refs[1:1 + n_y]
    w_refs = refs[1 + n_y:1 + 2 * n_y]
    nx_ref, wq_ref, k_ref, v_ref, wo_ref = refs[1 + 2 * n_y:6 + 2 * n_y]
    o_ref = refs[6 + 2 * n_y + (2 if route else 0)]
    h = res_ref[...]
    for y_ref, w_ref in zip(y_refs, w_refs):
        h = h + _dot(y_ref[...], w_ref[...])
    d = h.shape[1]
    hd = d // XA_HEADS
    q = (_dot(_rms(h, nx_ref[...]).astype(BF16), wq_ref[...]) * (hd ** -0.5)).astype(BF16)
    k = k_ref[0]
    v = v_ref[0]
    outs = []
    for i in range(XA_HEADS):
        hs = slice(i * hd, (i + 1) * hd)
        s = lax.dot_general(q[:, hs], k[:, hs], (((1,), (1,)), ((), ())),
                            preferred_element_type=F32)
        p = jnp.exp(s - jnp.max(s, axis=-1, keepdims=True))
        p = p / jnp.sum(p, axis=-1, keepdims=True)
        outs.append(_dot(p.astype(BF16), v[:, hs]).astype(BF16))
    o = jnp.concatenate(outs, axis=1)
    h = h + _dot(o, wo_ref[...])
    o_ref[...] = h
    if route:
        rnw_ref, r_ref = refs[6 + 2 * n_y:8 + 2 * n_y]
        meta_ref, cnt_ref, run_scr = refs[9 + 2 * n_y:]
        _route(h, rnw_ref, r_ref, meta_ref, cnt_ref, run_scr)


def _out_xattn(res, ys, ws, norm_w, wq, k, v, wo, seq_len, route=None):
    n, d = res.shape
    tm = ROW_TILE
    nm = k.shape[1]
    tiles_per_seq = seq_len // tm
    row = lambda w: pl.BlockSpec((tm, w), lambda i: (i, 0))
    const = lambda a: pl.BlockSpec(a.shape, lambda i: (0, 0))
    kv = pl.BlockSpec((1, nm, d), lambda i: (i // tiles_per_seq, 0, 0))
    args = [res, *ys, *ws, norm_w.reshape(1, d), wq, k, v, wo]
    in_specs = ([row(d)] + [row(y.shape[1]) for y in ys] + [const(w) for w in ws]
                + [pl.BlockSpec((1, d), lambda i: (0, 0)), const(wq), kv, kv, const(wo)])
    out_shape = jax.ShapeDtypeStruct((n, d), F32)
    out_specs = row(d)
    scratch = []
    if route is not None:
        route_norm, router = route
        router_t = router.T
        ne = router_t.shape[0]
        assert ne == SUBLANES, "routing keeps one expert per sublane"
        args += [route_norm.reshape(1, d), router_t]
        in_specs += [pl.BlockSpec((1, d), lambda i: (0, 0)), const(router_t)]
        out_shape = (out_shape, jax.ShapeDtypeStruct((ne, n), F32),
                     jax.ShapeDtypeStruct((ne, LANES), F32))
        out_specs = (out_specs, pl.BlockSpec((ne, tm), lambda i: (0, i)),
                     pl.BlockSpec((ne, LANES), lambda i: (0, 0)))
        scratch = [pltpu.VMEM((ne, LANES), F32)]
    return pl.pallas_call(
        functools.partial(_out_xattn_kernel, len(ys), route is not None),
        out_shape=out_shape,
        grid=(n // tm,),
        in_specs=in_specs,
        out_specs=out_specs,
        scratch_shapes=scratch,
        compiler_params=_params(("arbitrary",)),
        name="out_xattn",
    )(*args)


def _ffn_kernel(h_ref, nw_ref, w1_ref, w3_ref, w2_ref, o_ref, hn_scr):
    f = pl.program_id(1)

    @pl.when(f == 0)
    def _():
        h = h_ref[...]
        hn_scr[...] = _rms(h, nw_ref[...]).astype(BF16)
        o_ref[...] = h

    hn = hn_scr[...]
    a = _dot(hn, w1_ref[...])
    c = _dot(hn, w3_ref[...])
    o_ref[...] += _dot((_silu(a) * c).astype(BF16), w2_ref[...])


def _ffn(h, norm_w, w1, w3, w2, n_chunks):
    n, d = h.shape
    f = w1.shape[1]
    fc = f // n_chunks
    tm = ROW_TILE
    return pl.pallas_call(
        _ffn_kernel,
        out_shape=jax.ShapeDtypeStruct((n, d), F32),
        grid=(n // tm, n_chunks),
        in_specs=[pl.BlockSpec((tm, d), lambda i, j: (i, 0)),
                  pl.BlockSpec((1, d), lambda i, j: (0, 0)),
                  pl.BlockSpec((d, fc), lambda i, j: (0, j)),
                  pl.BlockSpec((d, fc), lambda i, j: (0, j)),
                  pl.BlockSpec((fc, d), lambda i, j: (j, 0))],
        out_specs=pl.BlockSpec((tm, d), lambda i, j: (i, 0)),
        scratch_shapes=[pltpu.VMEM((tm, d), BF16)],
        compiler_params=_params(("arbitrary", "arbitrary")),
        name="ffn",
    )(h, norm_w.reshape(1, d), w1, w3, w2)


def _softplus(x):
    return jnp.maximum(x, 0.0) + jnp.log1p(jnp.exp(-jnp.abs(x)))


def _m2_in_kernel(tiles_per_seq, x_ref, nw_ref, w_ref, cw_ref, cb_ref, zs_ref, xc_ref, dt_ref,
                  tail_scr):
    tm = x_ref.shape[0]
    pad = SUBLANES
    cdim = xc_ref.shape[1]
    hn = _rms(x_ref[...], nw_ref[...]).astype(BF16)

    @pl.when(pl.program_id(0) % tiles_per_seq == 0)
    def _():
        tail_scr[...] = jnp.zeros_like(tail_scr)

    blk = 4 * LANES
    for c0 in range(0, cdim, blk):
        cs = slice(c0, c0 + blk)
        xbc = _dot(hn, w_ref[:, M2_DINNER + c0:M2_DINNER + c0 + blk])
        if c0 < M2_DINNER:
            zs_ref[:, cs] = _silu(_dot(hn, w_ref[:, cs])).astype(zs_ref.dtype)
        ext = jnp.concatenate([tail_scr[:, cs], xbc], axis=0)
        acc = cb_ref[:, cs] + cw_ref[M2_CONV - 1:M2_CONV, cs] * xbc
        for j in range(M2_CONV - 1):
            shifted = pltpu.roll(ext, M2_CONV - 1 - j, 0)[pad:pad + tm, :]
            acc = acc + cw_ref[j:j + 1, cs] * shifted
        xc_ref[:, cs] = _silu(acc).astype(xc_ref.dtype)
        tail_scr[:, cs] = xbc[tm - pad:, :]
    dt_ref[...] = _dot(hn, w_ref[:, M2_DINNER + cdim:])


def _m2_in_proj(x, norm_w, w, conv_w, conv_b, seq_len):
    n, d = x.shape
    cdim = conv_w.shape[1]
    tm = ROW_TILE
    row = lambda wd: pl.BlockSpec((tm, wd), lambda i: (i, 0))
    const = lambda a: pl.BlockSpec(a.shape, lambda i: (0, 0))
    conv_b = conv_b.reshape(1, cdim)
    return pl.pallas_call(
        functools.partial(_m2_in_kernel, seq_len // tm),
        out_shape=(jax.ShapeDtypeStruct((n, M2_DINNER), BF16),
                   jax.ShapeDtypeStruct((n, cdim), BF16),
                   jax.ShapeDtypeStruct((n, M2_HEADS), F32)),
        grid=(n // tm,),
        in_specs=[row(d), pl.BlockSpec((1, d), lambda i: (0, 0)), const(w), const(conv_w),
                  const(conv_b)],
        out_specs=(row(M2_DINNER), row(cdim), row(M2_HEADS)),
        scratch_shapes=[pltpu.VMEM((SUBLANES, cdim), F32)],
        compiler_params=_params(("arbitrary",)),
        name="m2_in_proj",
    )(x, norm_w.reshape(1, d), w, conv_w, conv_b)


def _ssd_kernel(zs_ref, xc_ref, dt_ref, dtt_ref, dtb_ref, dtbt_ref, alog_ref, alogt_ref,
                dsk_ref, nw_ref, o_ref, state_scr, y_scr):
    t = M2_CHUNK

    @pl.when(pl.program_id(1) == 0)
    def _():
        state_scr[...] = jnp.zeros_like(state_scr)

    xc = xc_ref[0]
    xs_bf = xc[:, :M2_DINNER]
    bm = xc[:, M2_DINNER:M2_DINNER + M2_BC]
    cm = xc[:, M2_DINNER + M2_BC:]

    dt = _softplus(dt_ref[0] + dtb_ref[...])
    adt = dt * (-jnp.exp(alog_ref[...]))
    dtt = _softplus(dtt_ref[0] + dtbt_ref[...])
    adtt = dtt * (-jnp.exp(alogt_ref[...]))
    ri = lax.broadcasted_iota(jnp.int32, (t, t), 0)
    ci = lax.broadcasted_iota(jnp.int32, (t, t), 1)
    causal = ri >= ci
    tri = causal.astype(F32)
    acum = jnp.dot(tri, adt, preferred_element_type=F32, precision=lax.Precision.HIGHEST)
    acumt = lax.dot_general(adtt, tri, (((1,), (1,)), ((), ())), preferred_element_type=F32,
                            precision=lax.Precision.HIGHEST)
    arow = acumt - jnp.log(dtt)
    alast = acum[t - 1:t, :]
    wdec = (dt * jnp.exp(alast - acum)).astype(BF16)
    eac = jnp.exp(acum).astype(BF16)
    sdec = jnp.exp(alast)

    state = state_scr[...]
    state_bf = state.astype(BF16)
    lane = lax.broadcasted_iota(jnp.int32, (t, M2_DINNER), 1)
    even = (lane % (2 * M2_HEADDIM)) < M2_HEADDIM
    zero = jnp.zeros((), BF16)
    rhs_half = (jnp.concatenate([jnp.where(even, xs_bf, zero), jnp.where(even, state_bf, zero)], axis=0),
                jnp.concatenate([jnp.where(even, zero, xs_bf), jnp.where(even, zero, state_bf)], axis=0))
    for g in range(M2_GROUPS):
        ns = slice(g * M2_STATE, (g + 1) * M2_STATE)
        cb = lax.dot_general(cm[:, ns], bm[:, ns], (((1,), (1,)), ((), ())),
                             preferred_element_type=F32).astype(BF16)
        wcols, scols = [], []
        for kp in range(M2_HPG // 2):
            pair = slice((g * M2_HPG + 2 * kp) * M2_HEADDIM, (g * M2_HPG + 2 * kp + 2) * M2_HEADDIM)
            y = None
            for half in range(2):
                h = g * M2_HPG + 2 * kp + half
                seg = acum[:, h:h + 1] - arow[h:h + 1, :]
                lmat = jnp.exp(jnp.where(causal, seg, -jnp.inf)).astype(BF16)
                coff = cm[:, ns] * eac[:, h:h + 1]
                lhs = jnp.concatenate([cb * lmat, coff], axis=1)
                part = _dot(lhs, rhs_half[half][:, pair])
                y = part if y is None else y + part
                wcols.append(jnp.broadcast_to(wdec[:, h:h + 1], (t, M2_HEADDIM)))
                scols.append(jnp.broadcast_to(sdec[:, h:h + 1], (1, M2_HEADDIM)))
            y_scr[:, pair] = y
        gs = slice(g * M2_HPG * M2_HEADDIM, (g + 1) * M2_HPG * M2_HEADDIM)
        xd = xs_bf[:, gs] * jnp.concatenate(wcols, axis=1)
        upd = lax.dot_general(bm[:, ns], xd, (((0,), (0,)), ((), ())),
                              preferred_element_type=F32)
        state_scr[:, gs] = state[:, gs] * jnp.concatenate(scols, axis=1) + upd

    yz = (y_scr[...] + dsk_ref[...] * xs_bf.astype(F32)) * zs_ref[0].astype(F32)
    o_ref[0] = _rms(yz, nw_ref[...]).astype(o_ref.dtype)


def _ssd(zs, xc, dt, dt_bias, a_log, d_skip, norm_w):
    b, l, _ = zs.shape
    t = M2_CHUNK
    cdim = xc.shape[2]
    dtt = jnp.swapaxes(dt, 1, 2)
    row = lambda a: a.reshape(1, -1)
    col = lambda a: a.reshape(-1, 1)
    seq = lambda w: pl.BlockSpec((1, t, w), lambda i, c: (i, c, 0))
    const = lambda a: pl.BlockSpec(a.shape, lambda i, c: (0, 0))
    args = [zs, xc, dt, dtt, row(dt_bias), col(dt_bias), row(a_log), col(a_log),
            row(jnp.repeat(d_skip, M2_HEADDIM)), row(norm_w)]
    in_specs = [seq(M2_DINNER), seq(cdim), seq(M2_HEADS),
                pl.BlockSpec((1, M2_HEADS, t), lambda i, c: (i, 0, c))]
    in_specs += [const(a) for a in args[4:]]
    return pl.pallas_call(
        _ssd_kernel,
        out_shape=jax.ShapeDtypeStruct((b, l, M2_DINNER), BF16),
        grid=(b, l // t),
        in_specs=in_specs,
        out_specs=seq(M2_DINNER),
        scratch_shapes=[pltpu.VMEM((M2_STATE, M2_DINNER), F32),
                        pltpu.VMEM((t, M2_DINNER), F32)],
        compiler_params=_params(("arbitrary", "arbitrary")),
        name="ssd",
    )(*args)


MOE_TILE = 512
META_I1, META_I2, META_R1, META_R2, META_G1, META_G2 = range(6)


def _route(h, nw_ref, r_ref, meta_ref, cnt_ref, run_scr):
    @pl.when(pl.program_id(0) == 0)
    def _():
        run_scr[...] = jnp.zeros_like(run_scr)

    tm = h.shape[0]
    ne = r_ref.shape[0]
    eidx = lax.broadcasted_iota(jnp.int32, (ne, tm), 0).astype(F32)
    nt = lambda a, b: lax.dot_general(a, b, (((1,), (1,)), ((), ())), preferred_element_type=F32)
    hn_hi, hn_lo = _split_bf16(_rms(h, nw_ref[...]))
    r_hi, r_lo = _split_bf16(r_ref[...])
    logits = nt(r_hi, hn_hi) + nt(r_lo, hn_hi) + nt(r_hi, hn_lo)
    m1 = jnp.max(logits, axis=0, keepdims=True)
    i1 = jnp.min(jnp.where(logits == m1, eidx, float(ne)), axis=0, keepdims=True)
    rest = jnp.where(eidx == i1, -jnp.inf, logits)
    m2 = jnp.max(rest, axis=0, keepdims=True)
    i2 = jnp.min(jnp.where(rest == m2, eidx, float(ne)), axis=0, keepdims=True)
    e2 = jnp.exp(m2 - m1)
    g1 = 1.0 / (1.0 + e2)
    g2 = e2 / (1.0 + e2)
    oh1 = (eidx == i1).astype(F32)
    oh2 = (eidx == i2).astype(F32)
    oh = oh1 + oh2
    ri = lax.broadcasted_iota(jnp.int32, (tm, tm), 0)
    ci = lax.broadcasted_iota(jnp.int32, (tm, tm), 1)
    before = _dot(oh.astype(BF16), (ri < ci).astype(BF16)) + run_scr[:, 0:1]
    r1 = jnp.sum(oh1 * before, axis=0, keepdims=True)
    r2 = jnp.sum(oh2 * before, axis=0, keepdims=True)
    run_scr[...] = run_scr[...] + jnp.sum(oh, axis=1, keepdims=True)
    cnt_ref[...] = run_scr[...]
    meta = jnp.zeros((ne, tm), F32)
    for row, val in ((META_I1, i1), (META_I2, i2), (META_R1, r1), (META_R2, r2),
                     (META_G1, g1), (META_G2, g2)):
        meta = jnp.where(eidx == float(row), val, meta)
    meta_ref[...] = meta


def _row_copy(src_hbm, src_row, dst_ref, dst_row, sem):
    return pltpu.make_async_copy(src_hbm.at[pl.ds(src_row, 1)], dst_ref.at[pl.ds(dst_row, 1)], sem)


def _gmm_kernel(n_chunks, te_ref, tv_ref, src_ref, nxt_ref, hn_hbm, nw_ref, w1_ref, w3_ref, w2_ref,
                o_ref, xg_scr, sems):
    del te_ref
    j = pl.program_id(0)
    tg = xg_scr.shape[1]
    fc = w1_ref.shape[2] // n_chunks
    slot = lax.rem(j, 2)

    @pl.when(j == 0)
    def _():
        def issue(r, carry):
            _row_copy(hn_hbm, src_ref[0, 0, r], xg_scr.at[0], r, sems.at[0]).start()
            return carry
        lax.fori_loop(0, tg, issue, 0, unroll=8)

    @pl.when(jnp.logical_or(j == 0, tv_ref[jnp.maximum(j - 1, 0)] != 0))
    def _():
        def drain(r, carry):
            _row_copy(hn_hbm, 0, xg_scr.at[slot], 0, sems.at[slot]).wait()
            return carry
        lax.fori_loop(0, tg, drain, 0, unroll=8)

    @pl.when(tv_ref[j] != 0)
    def _():
        for r in range(tg):
            _row_copy(hn_hbm, nxt_ref[0, 0, r], xg_scr.at[1 - slot], r, sems.at[1 - slot]).start()
        x = _rms(xg_scr[slot], nw_ref[...]).astype(BF16)
        for c in range(n_chunks):
            fs = slice(c * fc, (c + 1) * fc)
            a = _dot(x, w1_ref[0, :, fs])
            b = _dot(x, w3_ref[0, :, fs])
            part = _dot((_silu(a) * b).astype(BF16), w2_ref[0, fs, :])
            if c == 0:
                o_ref[...] = part
            else:
                o_ref[...] += part

    @pl.when(tv_ref[j] == 0)
    def _():
        o_ref[...] = jnp.zeros_like(o_ref)


def _grouped_swiglu(hn, norm_w, src, tile_expert, tile_valid, w1, w3, w2, n_chunks):
    d = hn.shape[1]
    f = w1.shape[2]
    tg = MOE_TILE
    n_tiles = src.shape[0] // tg
    src = src.reshape(n_tiles, 1, tg)
    expert = lambda shape: pl.BlockSpec(shape, lambda j, te, tv: (te[j], 0, 0),
                                        pipeline_mode=pl.Buffered(1))
    grid_spec = pltpu.PrefetchScalarGridSpec(
        num_scalar_prefetch=2,
        grid=(n_tiles,),
        in_specs=[pl.BlockSpec((1, 1, tg), lambda j, te, tv: (j, 0, 0), memory_space=pltpu.SMEM),
                  pl.BlockSpec((1, 1, tg), lambda j, te, tv: (jnp.minimum(j + 1, n_tiles - 1), 0, 0),
                               memory_space=pltpu.SMEM),
                  pl.BlockSpec(memory_space=pl.ANY),
                  pl.BlockSpec((1, d), lambda j, te, tv: (0, 0)),
                  expert((1, d, f)), expert((1, d, f)), expert((1, f, d))],
        out_specs=pl.BlockSpec((tg, d), lambda j, te, tv: (j, 0)),
        scratch_shapes=[pltpu.VMEM((2, tg, d), F32), pltpu.SemaphoreType.DMA((2,))],
    )
    return pl.pallas_call(
        functools.partial(_gmm_kernel, n_chunks),
        out_shape=jax.ShapeDtypeStruct((n_tiles * tg, d), F32),
        grid_spec=grid_spec,
        compiler_params=_params(("arbitrary",)),
        name="moe_gmm",
    )(tile_expert, tile_valid, src, src, hn, norm_w.reshape(1, d), w1, w3, w2)


def _combine_kernel(pos1_ref, pos2_ref, h_ref, meta_ref, fw_ref, ys_hbm, o_ref,
                    y1_scr, y2_scr, sems):
    rows = h_ref.shape[0]
    for t in range(rows):
        _row_copy(ys_hbm, pos1_ref[0, 0, t], y1_scr, t, sems.at[0]).start()
        _row_copy(ys_hbm, pos2_ref[0, 0, t], y2_scr, t, sems.at[1]).start()
    for t in range(rows):
        _row_copy(ys_hbm, 0, y1_scr, t, sems.at[0]).wait()
        _row_copy(ys_hbm, 0, y2_scr, t, sems.at[1]).wait()
    gates = meta_ref[...]
    o_ref[...] = _rms(h_ref[...] + gates[:, 0:1] * y1_scr[...] + gates[:, 1:2] * y2_scr[...],
                      fw_ref[...])


def _combine(h, meta, pos1, pos2, ys, final_w):
    n, d = h.shape
    tm = ROW_TILE
    steps = n // tm
    idx = lambda: pl.BlockSpec((1, 1, tm), lambda i: (i, 0, 0), memory_space=pltpu.SMEM)
    return pl.pallas_call(
        _combine_kernel,
        out_shape=jax.ShapeDtypeStruct((n, d), F32),
        grid=(steps,),
        in_specs=[idx(), idx(),
                  pl.BlockSpec((tm, d), lambda i: (i, 0)),
                  pl.BlockSpec((tm, meta.shape[1]), lambda i: (i, 0)),
                  pl.BlockSpec((1, d), lambda i: (0, 0)),
                  pl.BlockSpec(memory_space=pl.ANY)],
        out_specs=pl.BlockSpec((tm, d), lambda i: (i, 0)),
        scratch_shapes=[pltpu.VMEM((tm, d), F32), pltpu.VMEM((tm, d), F32),
                        pltpu.SemaphoreType.DMA((2,))],
        compiler_params=_params(("arbitrary",)),
        name="moe_combine",
    )(pos1.reshape(steps, 1, tm), pos2.reshape(steps, 1, tm), h, meta,
      final_w.reshape(1, d), ys)


def _moe(h, meta, counts, norm_w, w1, w3, w2, final_w, n_chunks):
    n, d = h.shape
    ne = w1.shape[0]
    tg = MOE_TILE
    counts = counts[:, 0].astype(jnp.int32)
    padded = ((counts + tg - 1) // tg) * tg
    ends = jnp.cumsum(padded)
    offs = ends - padded
    col = lambda c: meta[c].astype(jnp.int32)
    pos1 = offs[col(META_I1)] + col(META_R1)
    pos2 = offs[col(META_I2)] + col(META_R2)
    n_tiles = (2 * n) // tg + ne + 1
    starts = jnp.arange(n_tiles, dtype=jnp.int32) * tg
    tile_expert = jnp.minimum(jnp.searchsorted(ends, starts, side='right'), ne - 1).astype(jnp.int32)
    tile_valid = (starts < ends[-1]).astype(jnp.int32)
    tok = jnp.arange(n, dtype=jnp.int32)
    src = jnp.zeros((n_tiles * tg,), jnp.int32).at[jnp.concatenate([pos1, pos2])].set(
        jnp.concatenate([tok, tok]), unique_indices=True)
    ys = _grouped_swiglu(h, norm_w, src, tile_expert, tile_valid, w1, w3, w2, n_chunks)
    gates = meta[META_G1:META_G2 + 1].T
    return _combine(h, gates, pos1, pos2, ys, final_w)


def kernel(x, mem, positions, mem_norm, norm_mix, norm_xattn, norm_ffn, xa_wq, xa_wk, xa_wv, xa_wo, ev_w_in, ev_s5_lam_re, ev_s5_lam_im, ev_s5_log_dt, ev_s5_b_re, ev_s5_b_im, ev_s5_c_re, ev_s5_c_im, ev_s5_d, ev_s5_w_glu, ev_s5_b_glu, ev_w_out, ev_ffn_w1, ev_ffn_w3, ev_ffn_w2, od_w_in, od_conv_w, od_conv_b, od_dt_bias, od_a_log, od_d, od_norm, od_w_out, od_router, od_moe_w1, od_moe_w3, od_moe_w2, final_norm):
    b, l, d = x.shape
    n = b * l
    assert l % ROW_TILE == 0 and l % RET_CHUNK == 0 and l % M2_CHUNK == 0 and l % S5_STEPS == 0
    assert b == SUBLANES, "the S5 scan maps the batch onto the sublanes"
    bf = lambda a: a.astype(BF16)

    mem_k, mem_v = _memkv(mem, mem_norm, bf(xa_wk), bf(xa_wv))
    h = x.reshape(n, d)

    q, k, v, g, u = _in_proj(
        h, norm_mix[0], bf(ev_w_in[0]),
        (RET_QK, RET_QK, RET_V, RET_V, S5_WIDTH), (F32, F32, BF16, F32, F32))
    y_ret = _retention(q.reshape(b, l, -1), k.reshape(b, l, -1), v.reshape(b, l, -1),
                       g.reshape(b, l, -1), positions)
    u_tb = jnp.swapaxes(u.reshape(b, l, S5_WIDTH), 0, 1).reshape(n, S5_WIDTH)
    tables = _s5_tables(ev_s5_lam_re[0], ev_s5_lam_im[0], ev_s5_log_dt[0], ev_s5_b_re[0],
                        ev_s5_b_im[0], ev_s5_c_re[0], ev_s5_c_im[0])
    y_s5 = _s5(u_tb, b, tables, ev_s5_d[0], bf(ev_s5_w_glu[0]), ev_s5_b_glu[0])
    y_s5 = jnp.swapaxes(y_s5.reshape(l, b, S5_WIDTH), 0, 1).reshape(n, S5_WIDTH)
    w_out = bf(ev_w_out[0])
    h = _out_xattn(h, [y_ret.reshape(n, RET_V), y_s5], [w_out[:RET_V], w_out[RET_V:]],
                   norm_xattn[0], bf(xa_wq[0]), mem_k[0], mem_v[0], bf(xa_wo[0]), l)
    h = _ffn(h, norm_ffn[0], bf(ev_ffn_w1[0]), bf(ev_ffn_w3[0]), bf(ev_ffn_w2[0]), 2)

    zs, xc, dt = _m2_in_proj(h, norm_mix[1], bf(od_w_in[0]), od_conv_w[0], od_conv_b[0], l)
    y = _ssd(zs.reshape(b, l, -1), xc.reshape(b, l, -1), dt.reshape(b, l, -1),
             od_dt_bias[0], od_a_log[0], od_d[0], od_norm[0])
    h, meta, counts = _out_xattn(h, [y.reshape(n, M2_DINNER)], [bf(od_w_out[0])],
                                 norm_xattn[1], bf(xa_wq[1]), mem_k[1], mem_v[1], bf(xa_wo[1]), l,
                                 route=(norm_ffn[1], od_router[0]))
    out = _moe(h, meta, counts, norm_ffn[1], bf(od_moe_w1[0]), bf(od_moe_w3[0]),
               bf(od_moe_w2[0]), final_norm, 2)
    return out.reshape(b, l, d)
```
